```python
import jax, jax.numpy as jnp
from jax import lax
import numpy as np

D_MODEL = 2048
BATCH = 4
SEQ = 2048
DEPTH = 4
DEC_BATCH = 128
DEC_SEQ = 4
PAST_LEN = 8192
PAGE_SIZE = 128

HEAD_DIM = 128
MIX_WIDTH = D_MODEL
RET_HEADS = MIX_WIDTH // (4 * HEAD_DIM)
M_HEADS = MIX_WIDTH // (4 * HEAD_DIM)
MLA_HEADS = MIX_WIDTH // (2 * HEAD_DIM)
RET_WIDTH = RET_HEADS * HEAD_DIM
M_WIDTH = M_HEADS * HEAD_DIM
QK_NOPE = HEAD_DIM
ROPE_DIM = HEAD_DIM // 2
V_DIM = HEAD_DIM
KV_RANK = 2 * HEAD_DIM
D_FF = 4 * D_MODEL
CONV_W = 4
RET_CHUNK = 128
MLSTM_CHUNK = 128
Q_BLOCK = 128
ROPE_BASE = 10000.0
EPS = 1e-6
MLA_SCALE = (QK_NOPE + ROPE_DIM) ** -0.5
IN_SIZES = (RET_WIDTH, RET_WIDTH, RET_WIDTH, RET_WIDTH,
            2 * M_WIDTH, M_WIDTH, M_WIDTH, M_HEADS, M_HEADS,
            MLA_HEADS * (QK_NOPE + ROPE_DIM), KV_RANK + ROPE_DIM)
N_IN = sum(IN_SIZES)

kernel_name = 'hymba_ret_mlstm_mla_step'


def rmsnorm(x, g):
    xf = x.astype(jnp.float32)
    y = xf * lax.rsqrt(jnp.mean(xf * xf, axis=-1, keepdims=True) + EPS)
    return (y * g.astype(jnp.float32)).astype(x.dtype)


def head_layernorm(y, g):
    B, L, H, d = y.shape
    yc = y - jnp.mean(y, axis=-1, keepdims=True)
    yn = yc * lax.rsqrt(jnp.mean(yc * yc, axis=-1, keepdims=True) + EPS)
    return yn.reshape(B, L, H * d) * g.astype(jnp.float32)


def rope(x, pos):
    d = x.shape[-1]
    inv = ROPE_BASE ** (-jnp.arange(0, d, 2, dtype=jnp.float32) / d)
    ang = pos.astype(jnp.float32)[:, None] * inv[None, :]
    shape = (1, ang.shape[0]) + (1,) * (x.ndim - 3) + (d // 2,)
    cos = jnp.cos(ang).reshape(shape)
    sin = jnp.sin(ang).reshape(shape)
    x1 = x[..., : d // 2].astype(jnp.float32)
    x2 = x[..., d // 2:].astype(jnp.float32)
    return jnp.concatenate([x1 * cos - x2 * sin, x2 * cos + x1 * sin], axis=-1).astype(x.dtype)


def causal_conv(u, buf, w, b):
    L = u.shape[1]
    xp = jnp.concatenate([buf.astype(u.dtype), u], axis=1)
    y = b
    for j in range(CONV_W):
        y = y + xp[:, j:j + L] * w[j]
    return y, xp[:, -(CONV_W - 1):]


def retention_chunkwise(q, k, v, S0):
    B, L, H, d = q.shape
    C = min(RET_CHUNK, L)
    N = L // C
    lg = jnp.log1p(-jnp.exp2(-5.0 - jnp.arange(H, dtype=jnp.float32)))
    idx = jnp.arange(C, dtype=jnp.float32)
    diff = idx[:, None] - idx[None, :]
    Dm = jnp.where(diff[None] >= 0, jnp.exp(jnp.maximum(diff, 0.0)[None] * lg[:, None, None]), 0.0)
    q_dec = jnp.exp((idx[:, None] + 1.0) * lg[None, :])
    k_dec = jnp.exp((C - 1.0 - idx)[:, None] * lg[None, :])
    c_dec = jnp.exp(C * lg)

    def to_chunks(t):
        return jnp.swapaxes(t.reshape(B, N, C, H, t.shape[-1]), 0, 1)

    def step(S, xs):
        qc, kc, vc = xs
        a = jnp.einsum('bihd,bjhd->bhij', qc, kc) * Dm
        o = (jnp.einsum('bhij,bjhe->bihe', a, vc)
             + jnp.einsum('bihd,bhde->bihe', qc * q_dec[None, :, :, None], S))
        S = S * c_dec[None, :, None, None] + jnp.einsum('bjhd,bjhe->bhde', kc * k_dec[None, :, :, None], vc)
        return S, o

    S, o = lax.scan(step, S0, (to_chunks(q), to_chunks(k), to_chunks(v)))
    return jnp.swapaxes(o, 0, 1).reshape(B, L, H, v.shape[-1]), S


def mlstm_chunkwise(q, k, v, ig, lf, C0, n0, m0):
    B, L, H, d = q.shape
    C = min(MLSTM_CHUNK, L)
    N = L // C
    causal = jnp.tril(jnp.ones((C, C), dtype=bool))

    def to_chunks(t):
        return jnp.swapaxes(t.reshape((B, N, C) + t.shape[2:]), 0, 1)

    def step(carry, xs):
        Cm, n, m = carry
        qc, kc, vc, ic, fc = xs
        bt = jnp.swapaxes(jnp.cumsum(fc, axis=1), 1, 2)
        it = jnp.swapaxes(ic, 1, 2)
        logw = jnp.where(causal, bt[..., :, None] - bt[..., None, :] + it[..., None, :], -jnp.inf)
        log_inter = bt + m[..., None]
        m_i = jnp.maximum(log_inter, jnp.max(logw, axis=-1))
        w = jnp.exp(logw - m_i[..., None])
        a = jnp.exp(log_inter - m_i)
        s = jnp.einsum('bihd,bjhd->bhij', qc, kc) * w
        num = jnp.einsum('bhij,bjhe->bhie', s, vc) + a[..., None] * jnp.einsum('bihd,bhde->bhie', qc, Cm)
        den = jnp.sum(s, axis=-1) + a * jnp.einsum('bihd,bhd->bhi', qc, n)
        h = num / jnp.maximum(jnp.abs(den), jnp.exp(-m_i))[..., None]
        m_new = m_i[..., -1]
        wl = jnp.exp(bt[..., -1:] - bt + it - m_new[..., None])
        al = jnp.exp(bt[..., -1] + m - m_new)
        Cm = al[..., None, None] * Cm + jnp.einsum('bjhd,bhj,bjhe->bhde', kc, wl, vc)
        n = al[..., None] * n + jnp.einsum('bjhd,bhj->bhd', kc, wl)
        return (Cm, n, m_new), jnp.swapaxes(h, 1, 2)

    (Cm, n, m), h = lax.scan(step, (C0, n0, m0),
                             (to_chunks(q), to_chunks(k), to_chunks(v), to_chunks(ig), to_chunks(lf)))
    return jnp.swapaxes(h, 0, 1).reshape(B, L, H, v.shape[-1]), Cm, n, m


def mla_attend(q_lat, q_pe, c, kr, q_pos, k_pos):
    s = (jnp.einsum('bthr,bsr->bhts', q_lat, c) + jnp.einsum('bthd,bsd->bhts', q_pe, kr)).astype(jnp.float32) * MLA_SCALE
    s = jnp.where(k_pos[None, None, None, :] <= q_pos[None, None, :, None], s, -jnp.inf)
    p = jax.nn.softmax(s, axis=-1).astype(c.dtype)
    return jnp.einsum('bhts,bsr->bthr', p, c)


def token_mixers(xn, pos, w_in, ret_gn, conv_w, conv_b, gate_b, m_gn, kv_norm, w_uk, w_uv, w_out,
                 ret_S0, C0, n0, m0, conv0, past_c, past_kr):
    B, L, _ = xn.shape
    f32 = jnp.float32
    proj = xn @ w_in
    offs = [int(o) for o in np.cumsum(IN_SIZES)[:-1]]
    rq, rk, rv, rg, mqk, mv, mo, mi, mf, mlq, mkv = jnp.split(proj, offs, axis=-1)

    def heads(t, H):
        return t.reshape(B, L, H, t.shape[-1] // H)

    rq_h = rope(heads(rq, RET_HEADS), pos).astype(f32)
    rk_h = rope(heads(rk, RET_HEADS), pos).astype(f32) * (HEAD_DIM ** -0.5)
    y_ret, ret_S = retention_chunkwise(rq_h, rk_h, heads(rv, RET_HEADS).astype(f32), ret_S0.astype(f32))
    y_ret = jax.nn.silu(rg.astype(f32)) * head_layernorm(y_ret, ret_gn)

    u, conv_state = causal_conv(mqk, conv0, conv_w, conv_b)
    u = jax.nn.silu(u.astype(f32))
    mq_h = heads(u[..., :M_WIDTH], M_HEADS)
    mk_h = heads(u[..., M_WIDTH:], M_HEADS) * (HEAD_DIM ** -0.5)
    gates = jnp.concatenate([mi, mf], axis=-1).astype(f32) + gate_b.astype(f32)
    ig = gates[..., :M_HEADS]
    lf = jax.nn.log_sigmoid(gates[..., M_HEADS:])
    h_m, Cm, n, m = mlstm_chunkwise(mq_h, mk_h, heads(mv, M_HEADS).astype(f32), ig, lf,
                                     C0.astype(f32), n0.astype(f32), m0.astype(f32))
    y_m = jax.nn.sigmoid(mo.astype(f32)) * head_layernorm(h_m, m_gn)

    mlq = heads(mlq, MLA_HEADS)
    q_nope = mlq[..., :QK_NOPE]
    q_pe = rope(mlq[..., QK_NOPE:], pos)
    c_new = rmsnorm(mkv[..., :KV_RANK], kv_norm)
    kr_new = rope(mkv[..., KV_RANK:], pos)
    q_lat = jnp.einsum('blhd,rhd->blhr', q_nope, w_uk)
    if past_c is None:
        c_all, kr_all, k_pos = c_new, kr_new, pos
    else:
        c_all = jnp.concatenate([past_c.astype(c_new.dtype), c_new], axis=1)
        kr_all = jnp.concatenate([past_kr.astype(kr_new.dtype), kr_new], axis=1)
        k_pos = jnp.concatenate([jnp.arange(past_c.shape[1], dtype=jnp.int32), pos])
    if L > Q_BLOCK and L % Q_BLOCK == 0:
        nb = L // Q_BLOCK

        def blk(t):
            return jnp.swapaxes(t.reshape((B, nb, Q_BLOCK) + t.shape[2:]), 0, 1)

        o = lax.map(lambda xs: mla_attend(xs[0], xs[1], c_all, kr_all, xs[2], k_pos),
                    (blk(q_lat), blk(q_pe), pos.reshape(nb, Q_BLOCK)))
        o = jnp.swapaxes(o, 0, 1).reshape(B, L, MLA_HEADS, KV_RANK)
    else:
        o = mla_attend(q_lat, q_pe, c_all, kr_all, pos, k_pos)
    y_mla = jnp.einsum('blhr,rhe->blhe', o, w_uv).reshape(B, L, MLA_HEADS * V_DIM)

    y = jnp.concatenate([y_ret.astype(xn.dtype), y_m.astype(xn.dtype), y_mla], axis=-1) @ w_out
    return y, (c_new, kr_new, ret_S, Cm, n, m, conv_state)


def mlp(x, w_up, w_down):
    h = jax.nn.relu(x @ w_up)
    return (h * h) @ w_down


def setup_inputs(seed: int = 0) -> dict:
    key = jax.random.key(seed)
    ks = jax.random.split(key, 28)
    f32 = jnp.float32

    def nrm(k, shape, scale):
        return jax.random.normal(k, shape, f32) * scale

    n_pages = PAST_LEN // PAGE_SIZE
    n_used = DEC_BATCH * n_pages
    n_pool = n_used + n_used // 4
    page_table = jax.random.permutation(ks[4], n_pool)[:n_used].reshape(DEC_BATCH, n_pages).astype(jnp.int32)
    gate_b = jnp.concatenate([nrm(ks[16], (DEPTH, M_HEADS), 0.1),
                              jnp.linspace(3.0, 6.0, M_HEADS, dtype=f32)[None, :] + nrm(ks[17], (DEPTH, M_HEADS), 0.1)], axis=-1)
    return {
        'x_prompt': nrm(ks[0], (BATCH, SEQ, D_MODEL), 1.0),
        'x_sample': nrm(ks[1], (DEC_BATCH, DEC_SEQ, D_MODEL), 1.0),
        'cache_mla_kv': nrm(ks[2], (DEPTH, n_pool, PAGE_SIZE, KV_RANK), 1.0),
        'cache_mla_kr': nrm(ks[3], (DEPTH, n_pool, PAGE_SIZE, ROPE_DIM), 1.0),
        'page_table': page_table,
        'state_ret': nrm(ks[5], (DEPTH, DEC_BATCH, RET_HEADS, HEAD_DIM, HEAD_DIM), 0.3),
        'state_mlstm_C': nrm(ks[6], (DEPTH, DEC_BATCH, M_HEADS, HEAD_DIM, HEAD_DIM), 0.3),
        'state_mlstm_n': nrm(ks[7], (DEPTH, DEC_BATCH, M_HEADS, HEAD_DIM), 0.3),
        'state_mlstm_m': nrm(ks[8], (DEPTH, DEC_BATCH, M_HEADS), 1.0),
        'state_mlstm_conv': nrm(ks[9], (DEPTH, DEC_BATCH, CONV_W - 1, 2 * M_WIDTH), 1.0),
        'norm_attn': 1.0 + nrm(ks[10], (DEPTH, D_MODEL), 0.02),
        'norm_mlp': 1.0 + nrm(ks[11], (DEPTH, D_MODEL), 0.02),
        'norm_final': 1.0 + nrm(ks[12], (D_MODEL,), 0.02),
        'w_in': nrm(ks[13], (DEPTH, D_MODEL, N_IN), D_MODEL ** -0.5),
        'ret_gn': 1.0 + nrm(ks[14], (DEPTH, RET_WIDTH), 0.02),
        'mlstm_conv_w': nrm(ks[15], (DEPTH, CONV_W, 2 * M_WIDTH), CONV_W ** -0.5),
        'mlstm_conv_b': nrm(ks[18], (DEPTH, 2 * M_WIDTH), 0.02),
        'mlstm_gate_b': gate_b,
        'mlstm_gn': 1.0 + nrm(ks[19], (DEPTH, M_WIDTH), 0.02),
        'mla_kv_norm': 1.0 + nrm(ks[20], (DEPTH, KV_RANK), 0.02),
        'w_uk': nrm(ks[21], (DEPTH, KV_RANK, MLA_HEADS, QK_NOPE), KV_RANK ** -0.5),
        'w_uv': nrm(ks[22], (DEPTH, KV_RANK, MLA_HEADS, V_DIM), KV_RANK ** -0.5),
        'w_out': nrm(ks[23], (DEPTH, MIX_WIDTH, D_MODEL), MIX_WIDTH ** -0.5),
        'w_up': nrm(ks[24], (DEPTH, D_MODEL, D_FF), D_MODEL ** -0.5),
        'w_down': nrm(ks[25], (DEPTH, D_FF, D_MODEL), D_FF ** -0.5),
    }


def reference(x_prompt, x_sample, cache_mla_kv, cache_mla_kr, page_table, state_ret, state_mlstm_C,
              state_mlstm_n, state_mlstm_m, state_mlstm_conv, norm_attn, norm_mlp, norm_final, w_in, ret_gn,
              mlstm_conv_w, mlstm_conv_b, mlstm_gate_b, mlstm_gn, mla_kv_norm, w_uk, w_uv, w_out, w_up, w_down):
    B, L, _ = x_prompt.shape
    DB, T, _ = x_sample.shape
    past_len = page_table.shape[1] * cache_mla_kv.shape[2]
    pos_p = jnp.arange(L, dtype=jnp.int32)
    pos_s = past_len + jnp.arange(T, dtype=jnp.int32)
    f32 = jnp.float32
    ret0 = jnp.zeros((B, RET_HEADS, HEAD_DIM, HEAD_DIM), f32)
    c0 = jnp.zeros((B, M_HEADS, HEAD_DIM, HEAD_DIM), f32)
    n0 = jnp.zeros((B, M_HEADS, HEAD_DIM), f32)
    m0 = jnp.zeros((B, M_HEADS), f32)
    conv0 = jnp.zeros((B, CONV_W - 1, 2 * M_WIDTH), x_prompt.dtype)

    hp, hs = x_prompt, x_sample
    new_p = [[] for _ in range(7)]
    new_s = [[] for _ in range(7)]
    for l in range(DEPTH):
        lw = (w_in[l], ret_gn[l], mlstm_conv_w[l], mlstm_conv_b[l], mlstm_gate_b[l], mlstm_gn[l],
              mla_kv_norm[l], w_uk[l], w_uv[l], w_out[l])
        yp, sp = token_mixers(rmsnorm(hp, norm_attn[l]), pos_p, *lw, ret0, c0, n0, m0, conv0, None, None)
        past_c = cache_mla_kv[l, page_table].reshape(DB, past_len, KV_RANK)
        past_kr = cache_mla_kr[l, page_table].reshape(DB, past_len, ROPE_DIM)
        ys, ss = token_mixers(rmsnorm(hs, norm_attn[l]), pos_s, *lw, state_ret[l], state_mlstm_C[l],
                              state_mlstm_n[l], state_mlstm_m[l], state_mlstm_conv[l], past_c, past_kr)
        hp = hp + yp
        hp = hp + mlp(rmsnorm(hp, norm_mlp[l]), w_up[l], w_down[l])
        hs = hs + ys
        hs = hs + mlp(rmsnorm(hs, norm_mlp[l]), w_up[l], w_down[l])
        for i in range(7):
            new_p[i].append(sp[i])
            new_s[i].append(ss[i])

    y_prompt = rmsnorm(hp, norm_final)
    y_sample = rmsnorm(hs, norm_final)
    return (y_prompt, y_sample,
            jnp.stack(new_p[0]), jnp.stack(new_p[1]), jnp.stack(new_s[0]), jnp.stack(new_s[1]),
            jnp.stack(new_p[2]), jnp.stack(new_s[2]), jnp.stack(new_p[3]), jnp.stack(new_s[3]),
            jnp.stack(new_p[4]), jnp.stack(new_s[4]), jnp.stack(new_p[5]), jnp.stack(new_s[5]),
            jnp.stack(new_p[6]), jnp.stack(new_s[6]))
```

```python
import functools

import numpy as np
import jax
import jax.numpy as jnp
from jax import lax
from jax.experimental import pallas as pl
from jax.experimental.pallas import tpu as pltpu

F32 = jnp.float32
BF16 = jnp.bfloat16

HEAD_DIM = 128
CONV_W = 4
CHUNK = 128
ROPE_BASE = 10000.0
EPS = 1e-6
QK_SCALE = HEAD_DIM ** -0.5
NEG_INF = float("-inf")

NT_DIMS = (((1,), (1,)), ((), ()))
TN_DIMS = (((0,), (0,)), ((), ()))


def _params(semantics, vmem_mb):
    return pltpu.CompilerParams(dimension_semantics=semantics, vmem_limit_bytes=vmem_mb << 20)


def _tile(n, target, mult):
    best = None
    for t in range(mult, min(n, target) + 1, mult):
        if n % t == 0:
            best = t
    assert best is not None, (n, target, mult)
    return best


def _rms(x, g):
    ms = jnp.mean(x * x, axis=-1, keepdims=True)
    return (x * lax.rsqrt(ms + EPS)) * g


def _head_ln(y, g):
    mu = jnp.mean(y, axis=-1, keepdims=True)
    yc = y - mu
    var = jnp.mean(yc * yc, axis=-1, keepdims=True)
    return (yc * lax.rsqrt(var + EPS)) * g


def _norm_matmul_kernel(x_ref, g_ref, w_ref, o_ref, xn_ref):
    @pl.when(pl.program_id(1) == 0)
    def _():
        xn_ref[...] = _rms(x_ref[...], g_ref[...]).astype(BF16)

    o_ref[...] = jnp.dot(xn_ref[...], w_ref[...], preferred_element_type=F32)


def norm_matmul(x, g, w):
    M, D = x.shape
    N = w.shape[1]
    tm = _tile(M, 1088, 16)
    tn = _tile(N, 512, 128)
    return pl.pallas_call(
        _norm_matmul_kernel,
        out_shape=jax.ShapeDtypeStruct((M, N), F32),
        grid=(M // tm, N // tn),
        in_specs=[
            pl.BlockSpec((tm, D), lambda i, j: (i, 0)),
            pl.BlockSpec((1, D), lambda i, j: (0, 0)),
            pl.BlockSpec((D, tn), lambda i, j: (0, j)),
        ],
        out_specs=pl.BlockSpec((tm, tn), lambda i, j: (i, j)),
        scratch_shapes=[pltpu.VMEM((tm, D), BF16)],
        compiler_params=_params(("parallel", "arbitrary"), 48),
        name="norm_inproj",
    )(x, g, w)


def _mlp_kernel(x_ref, g_ref, wu_ref, wd_ref, o_ref, xn_ref):
    @pl.when(pl.program_id(1) == 0)
    def _():
        x = x_ref[...]
        xn_ref[...] = _rms(x, g_ref[...]).astype(BF16)
        o_ref[...] = x

    u = jnp.dot(xn_ref[...], wu_ref[...], preferred_element_type=F32)
    a = jnp.maximum(u, 0.0)
    o_ref[...] += jnp.dot((a * a).astype(BF16), wd_ref[...], preferred_element_type=F32)


def mlp_residual(x, g, w_up, w_down):
    M, D = x.shape
    F = w_up.shape[1]
    tm = _tile(M, 544, 16)
    tf = _tile(F, 512, 128)
    return pl.pallas_call(
        _mlp_kernel,
        out_shape=jax.ShapeDtypeStruct((M, D), F32),
        grid=(M // tm, F // tf),
        in_specs=[
            pl.BlockSpec((tm, D), lambda i, f: (i, 0)),
            pl.BlockSpec((1, D), lambda i, f: (0, 0)),
            pl.BlockSpec((D, tf), lambda i, f: (0, f)),
            pl.BlockSpec((tf, D), lambda i, f: (f, 0)),
        ],
        out_specs=pl.BlockSpec((tm, D), lambda i, f: (i, 0)),
        scratch_shapes=[pltpu.VMEM((tm, D), BF16)],
        compiler_params=_params(("parallel", "arbitrary"), 48),
        name="mlp",
    )(x, g, w_up, w_down)


def _outproj_kernel(x_ref, yr_ref, ym_ref, o_ref, wuv_ref, w_ref, out_ref, ymla_ref, *, n_mla, w_ret, w_m):
    lat = wuv_ref.shape[1]
    for h in range(n_mla):
        yh = jnp.dot(o_ref[:, h * lat:(h + 1) * lat], wuv_ref[h], preferred_element_type=F32)
        ymla_ref[:, h * HEAD_DIM:(h + 1) * HEAD_DIM] = yh.astype(BF16)
    acc = x_ref[...]
    acc += jnp.dot(yr_ref[...], w_ref[0:w_ret, :], preferred_element_type=F32)
    acc += jnp.dot(ym_ref[...], w_ref[w_ret:w_ret + w_m, :], preferred_element_type=F32)
    acc += jnp.dot(ymla_ref[...], w_ref[w_ret + w_m:, :], preferred_element_type=F32)
    out_ref[...] = acc


def outproj_residual(x, y_ret, y_m, o_lat, w_uv, w_out):
    M, D = x.shape
    n_mla, lat, _ = w_uv.shape
    w_ret, w_m = y_ret.shape[1], y_m.shape[1]
    tm = _tile(M, 272, 16)
    kern = functools.partial(_outproj_kernel, n_mla=n_mla, w_ret=w_ret, w_m=w_m)
    return pl.pallas_call(
        kern,
        out_shape=jax.ShapeDtypeStruct((M, D), F32),
        grid=(M // tm,),
        in_specs=[
            pl.BlockSpec((tm, D), lambda i: (i, 0)),
            pl.BlockSpec((tm, w_ret), lambda i: (i, 0)),
            pl.BlockSpec((tm, w_m), lambda i: (i, 0)),
            pl.BlockSpec((tm, n_mla * lat), lambda i: (i, 0)),
            pl.BlockSpec(w_uv.shape, lambda i: (0, 0, 0)),
            pl.BlockSpec(w_out.shape, lambda i: (0, 0)),
        ],
        out_specs=pl.BlockSpec((tm, D), lambda i: (i, 0)),
        scratch_shapes=[pltpu.VMEM((tm, n_mla * HEAD_DIM), BF16)],
        compiler_params=_params(("parallel",), 48),
        name="outproj",
    )(x, y_ret, y_m, o_lat, w_uv, w_out)


def _final_norm_kernel(x_ref, g_ref, o_ref):
    o_ref[...] = _rms(x_ref[...], g_ref[...])


def final_norm(x, g):
    M, D = x.shape
    tm = _tile(M, 544, 8)
    return pl.pallas_call(
        _final_norm_kernel,
        out_shape=jax.ShapeDtypeStruct((M, D), F32),
        grid=(M // tm,),
        in_specs=[pl.BlockSpec((tm, D), lambda i: (i, 0)), pl.BlockSpec((1, D), lambda i: (0, 0))],
        out_specs=pl.BlockSpec((tm, D), lambda i: (i, 0)),
        compiler_params=_params(("parallel",), 32),
        name="final_norm",
    )(x, g)


def _rope128(x, cos2, sin2):
    return x * cos2 + pltpu.roll(x, HEAD_DIM // 2, 1) * sin2


def _ret_block(q_ref, k_ref, v_ref, g_ref, cos_ref, sin_ref, dm_ref, qd_ref, kd_ref, gn_ref, y_ref,
               load_state, store_state, rowseq, n_heads, n_sub, cdec):
    cos2, sin2 = cos_ref[...], sin_ref[...]
    for h in range(n_heads):
        sl = slice(h * HEAD_DIM, (h + 1) * HEAD_DIM)
        qr = _rope128(q_ref[:, sl], cos2, sin2)
        kr = _rope128(k_ref[:, sl], cos2, sin2) * QK_SCALE
        vb = v_ref[:, sl].astype(BF16)
        a = lax.dot_general(qr.astype(BF16), kr.astype(BF16), NT_DIMS, preferred_element_type=F32) * dm_ref[h]
        o = jnp.dot(a.astype(BF16), vb, preferred_element_type=F32)
        qs = (qr * qd_ref[:, sl]).astype(BF16)
        ks = kr * kd_ref[:, sl]
        for i in range(n_sub):
            s_old = load_state(i, h)
            inter = jnp.dot(qs, s_old.astype(BF16), preferred_element_type=F32)
            if n_sub == 1:
                o = o + inter
                ks_i = ks
            else:
                sel = rowseq == float(i)
                o = o + jnp.where(sel, inter, 0.0)
                ks_i = jnp.where(sel, ks, 0.0)
            kv = lax.dot_general(ks_i.astype(BF16), vb, TN_DIMS, preferred_element_type=F32)
            store_state(i, h, s_old * cdec[h] + kv)
        yn = _head_ln(o, gn_ref[:, sl])
        y_ref[:, sl] = (jax.nn.silu(g_ref[:, sl]) * yn).astype(BF16)


def _ret_prompt_kernel(q_ref, k_ref, v_ref, g_ref, cos_ref, sin_ref, dm_ref, qd_ref, kd_ref, gn_ref,
                       y_ref, so_ref, s_ref, *, n_heads, cdec):
    c = pl.program_id(1)

    @pl.when(c == 0)
    def _():
        s_ref[...] = jnp.zeros_like(s_ref)

    def load_state(i, h):
        return s_ref[h]

    def store_state(i, h, val):
        s_ref[h] = val

    _ret_block(q_ref, k_ref, v_ref, g_ref, cos_ref, sin_ref, dm_ref, qd_ref, kd_ref, gn_ref, y_ref,
               load_state, store_state, None, n_heads, 1, cdec)

    @pl.when(c == pl.num_programs(1) - 1)
    def _():
        so_ref[...] = s_ref[...]


def _ret_sample_kernel(q_ref, k_ref, v_ref, g_ref, cos_ref, sin_ref, dm_ref, qd_ref, kd_ref, gn_ref,
                       rs_ref, s0_ref, y_ref, so_ref, *, n_heads, n_sub, cdec):
    def load_state(i, h):
        return s0_ref[i, h]

    def store_state(i, h, val):
        so_ref[i, h] = val

    _ret_block(q_ref, k_ref, v_ref, g_ref, cos_ref, sin_ref, dm_ref, qd_ref, kd_ref, gn_ref, y_ref,
               load_state, store_state, rs_ref[...], n_heads, n_sub, cdec)


def _ret_decay_tables(n_heads, chunk, n_sub):
    lg = np.log1p(-np.exp2(-5.0 - np.arange(n_heads, dtype=np.float64)))
    idx = np.arange(chunk, dtype=np.float64)
    diff = idx[:, None] - idx[None, :]
    dm1 = np.where(diff[None] >= 0, np.exp(np.maximum(diff, 0.0)[None] * lg[:, None, None]), 0.0)
    rows = n_sub * chunk
    dm = np.zeros((n_heads, rows, rows))
    for i in range(n_sub):
        dm[:, i * chunk:(i + 1) * chunk, i * chunk:(i + 1) * chunk] = dm1
    q_dec = np.exp((idx[:, None] + 1.0) * lg[None, :])
    k_dec = np.exp((chunk - 1.0 - idx)[:, None] * lg[None, :])
    qd = np.tile(np.repeat(q_dec, HEAD_DIM, axis=1), (n_sub, 1))
    kd = np.tile(np.repeat(k_dec, HEAD_DIM, axis=1), (n_sub, 1))
    cdec = tuple(float(np.float32(v)) for v in np.exp(chunk * lg))
    return jnp.asarray(dm, F32), jnp.asarray(qd, F32), jnp.asarray(kd, F32), cdec


def retention_prompt(proj, cos2, sin2, gn, B, L, n_heads):
    W = n_heads * HEAD_DIM
    C = min(CHUNK, L)
    nc = L // C
    dm, qd, kd, cdec = _ret_decay_tables(n_heads, C, 1)
    kern = functools.partial(_ret_prompt_kernel, n_heads=n_heads, cdec=cdec)
    col = lambda j: pl.BlockSpec((C, W), lambda b, c, j=j: (b * nc + c, j))
    const2 = lambda shape: pl.BlockSpec(shape, lambda b, c: (0,) * len(shape))
    return pl.pallas_call(
        kern,
        out_shape=(jax.ShapeDtypeStruct((B * L, W), BF16),
                   jax.ShapeDtypeStruct((B, n_heads, HEAD_DIM, HEAD_DIM), F32)),
        grid=(B, nc),
        in_specs=[col(0), col(1), col(2), col(3),
                  pl.BlockSpec((C, HEAD_DIM), lambda b, c: (b * nc + c, 0)),
                  pl.BlockSpec((C, HEAD_DIM), lambda b, c: (b * nc + c, 0)),
                  const2(dm.shape), const2(qd.shape), const2(kd.shape), const2((1, W))],
        out_specs=(pl.BlockSpec((C, W), lambda b, c: (b * nc + c, 0)),
                   pl.BlockSpec((None, n_heads, HEAD_DIM, HEAD_DIM), lambda b, c: (b, 0, 0, 0))),
        scratch_shapes=[pltpu.VMEM((n_heads, HEAD_DIM, HEAD_DIM), F32)],
        compiler_params=_params(("parallel", "arbitrary"), 32),
        name="retention_prompt",
    )(proj, proj, proj, proj, cos2, sin2, dm, qd, kd, gn)


def retention_sample(proj, cos2, sin2, gn, state, layer, row0, DB, T, n_heads, G):
    W = n_heads * HEAD_DIM
    R = G * T
    rb0 = row0 // R
    dm, qd, kd, cdec = _ret_decay_tables(n_heads, T, G)
    rowseq = jnp.asarray(np.repeat(np.arange(G, dtype=np.float32), T)[:, None] * np.ones((1, HEAD_DIM), np.float32))
    kern = functools.partial(_ret_sample_kernel, n_heads=n_heads, n_sub=G, cdec=cdec)
    col = lambda j: pl.BlockSpec((R, W), lambda g, j=j: (rb0 + g, j))
    const1 = lambda shape: pl.BlockSpec(shape, lambda g: (0,) * len(shape))
    return pl.pallas_call(
        kern,
        out_shape=(jax.ShapeDtypeStruct((DB * T, W), BF16),
                   jax.ShapeDtypeStruct((DB, n_heads, HEAD_DIM, HEAD_DIM), F32)),
        grid=(DB // G,),
        in_specs=[col(0), col(1), col(2), col(3),
                  pl.BlockSpec((R, HEAD_DIM), lambda g: (rb0 + g, 0)),
                  pl.BlockSpec((R, HEAD_DIM), lambda g: (rb0 + g, 0)),
                  const1(dm.shape), const1(qd.shape), const1(kd.shape), const1((1, W)),
                  const1(rowseq.shape),
                  pl.BlockSpec((None, G, n_heads, HEAD_DIM, HEAD_DIM), lambda g: (layer, g, 0, 0, 0))],
        out_specs=(pl.BlockSpec((R, W), lambda g: (g, 0)),
                   pl.BlockSpec((G, n_heads, HEAD_DIM, HEAD_DIM), lambda g: (g, 0, 0, 0))),
        compiler_params=_params(("parallel",), 32),
        name="retention_sample",
    )(proj, proj, proj, proj, cos2, sin2, dm, qd, kd, gn, rowseq, state)


def _mlstm_block(u, v_ref, o_ref, gates, mprev_ref, mout_ref, nrows, tri, eye, lastm, gn_ref, y_ref,
                 load_c, store_c, store_n, rowseq, n_heads, n_sub, seq_len):
    W = n_heads * HEAD_DIM
    for h in range(n_heads):
        sl = slice(h * HEAD_DIM, (h + 1) * HEAD_DIM)
        q = u[:, h * HEAD_DIM:(h + 1) * HEAD_DIM]
        k = u[:, W + h * HEAD_DIM:W + (h + 1) * HEAD_DIM] * QK_SCALE
        qb, kb = q.astype(BF16), k.astype(BF16)
        vb = v_ref[:, sl].astype(BF16)
        ig_col = gates[:, h:h + 1]
        lf_col = gates[:, n_heads + h:n_heads + h + 1]
        m_col = mprev_ref[:, h:h + 1]
        f_row = jnp.sum(jnp.where(eye, lf_col, 0.0), axis=0, keepdims=True)
        bt_col = jnp.sum(jnp.where(tri, f_row, 0.0), axis=1, keepdims=True)
        bt_row = jnp.sum(jnp.where(eye, bt_col, 0.0), axis=0, keepdims=True)
        it_row = jnp.sum(jnp.where(eye, ig_col, 0.0), axis=0, keepdims=True)
        logw = jnp.where(tri, bt_col - bt_row + it_row, NEG_INF)
        log_inter = bt_col + m_col
        m_i = jnp.maximum(log_inter, jnp.max(logw, axis=1, keepdims=True))
        w = jnp.exp(logw - m_i)
        a = jnp.exp(log_inter - m_i)
        s = lax.dot_general(qb, kb, NT_DIMS, preferred_element_type=F32) * w
        num = jnp.dot(s.astype(BF16), vb, preferred_element_type=F32)
        den = jnp.sum(s, axis=1, keepdims=True) + a * jnp.sum(q * nrows(h), axis=1, keepdims=True)
        m_row = jnp.sum(jnp.where(eye, m_i, 0.0), axis=0, keepdims=True)
        m_last = jnp.sum(jnp.where(lastm, m_row, 0.0), axis=1, keepdims=True)
        bt_last = jnp.sum(jnp.where(lastm, bt_row, 0.0), axis=1, keepdims=True)
        wl = jnp.exp(bt_last - bt_col + ig_col - m_last)
        al = jnp.exp(bt_last + m_col - m_last)
        kw = k * wl
        inter_sum = None
        for i in range(n_sub):
            c_old = load_c(i, h)
            inter = jnp.dot(qb, c_old.astype(BF16), preferred_element_type=F32)
            if n_sub == 1:
                kw_i = kw
            else:
                sel = rowseq == float(i)
                inter = jnp.where(sel, inter, 0.0)
                kw_i = jnp.where(sel, kw, 0.0)
            inter_sum = inter if inter_sum is None else inter_sum + inter
            al_i = al[i * seq_len:i * seq_len + 1, :]
            kv = lax.dot_general(kw_i.astype(BF16), vb, TN_DIMS, preferred_element_type=F32)
            store_c(i, h, al_i * c_old + kv)
            store_n(i, h, al_i, jnp.sum(kw_i, axis=0, keepdims=True))
        num = num + a * inter_sum
        hh = num / jnp.maximum(jnp.abs(den), jnp.exp(-m_i))
        yn = _head_ln(hh, gn_ref[:, sl])
        y_ref[:, sl] = (jax.nn.sigmoid(o_ref[:, sl]) * yn).astype(BF16)
        mout_ref[:, h:h + 1] = m_last


def _gate_values(g_ref, gb_ref, n_heads):
    pre = g_ref[...] + gb_ref[...]
    lane = lax.broadcasted_iota(jnp.int32, pre.shape, 1)
    lf = jnp.minimum(pre, 0.0) - jnp.log1p(jnp.exp(-jnp.abs(pre)))
    return jnp.where(lane < n_heads, pre, lf)


def _conv_silu(taps, cw_ref, cb_ref):
    acc = cb_ref[...]
    for j in range(CONV_W):
        acc = acc + taps[j] * cw_ref[j:j + 1, :]
    return jax.nn.silu(acc)


def _mlstm_prompt_kernel(x_ref, v_ref, o_ref, g_ref, cw_ref, cb_ref, gb_ref, gn_ref, tri_ref, eye_ref,
                         last_ref, y_ref, co_ref, no_ref, mo_ref,
                         xbuf_ref, c_ref, n_ref, m_ref, *, n_heads):
    c = pl.program_id(1)
    C = x_ref.shape[0]

    @pl.when(c == 0)
    def _():
        xbuf_ref[0:8, :] = jnp.zeros((8, xbuf_ref.shape[1]), F32)
        c_ref[...] = jnp.zeros_like(c_ref)
        n_ref[...] = jnp.zeros_like(n_ref)
        m_ref[...] = jnp.zeros_like(m_ref)

    xbuf_ref[8:8 + C, :] = x_ref[...]
    taps = [xbuf_ref[8 - (CONV_W - 1) + j:8 - (CONV_W - 1) + j + C, :] for j in range(CONV_W)]
    u = _conv_silu(taps, cw_ref, cb_ref)
    xbuf_ref[0:8, :] = x_ref[C - 8:C, :]
    gates = _gate_values(g_ref, gb_ref, n_heads)
    tri, eye, lastm = tri_ref[...] > 0.5, eye_ref[...] > 0.5, last_ref[...] > 0.5

    def load_c(i, h):
        return c_ref[h]

    def store_c(i, h, val):
        c_ref[h] = val

    def store_n(i, h, al_i, ksum):
        n_ref[h] = al_i * n_ref[h] + ksum

    def nrows(h):
        return n_ref[h]

    _mlstm_block(u, v_ref, o_ref, gates, m_ref, m_ref, nrows, tri, eye, lastm, gn_ref, y_ref,
                 load_c, store_c, store_n, None, n_heads, 1, C)

    @pl.when(c == pl.num_programs(1) - 1)
    def _():
        co_ref[...] = c_ref[...]
        no_ref[...] = n_ref[...]
        mo_ref[...] = m_ref[0:8, :]


def _mlstm_sample_kernel(x_ref, v_ref, o_ref, g_ref, cw_ref, cb_ref, gb_ref, gn_ref, tri_ref, eye_ref,
                         last_ref, rs_ref, rt_ref, st_ref, c0_ref, n0_ref, nr_ref, m0_ref,
                         y_ref, co_ref, no_ref, mo_ref, *, n_heads, n_sub, seq_len):
    R = x_ref.shape[0]
    x = x_ref[...]
    st = st_ref[...]
    tpos = rt_ref[:, 0:1]
    taps = []
    for j in range(CONV_W):
        s = CONV_W - 1 - j
        if s == 0:
            taps.append(x)
        else:
            cur = pltpu.roll(x, s, 0)
            old = pltpu.roll(st, (s - seq_len) % R, 0)
            taps.append(jnp.where(tpos >= float(s), cur, old))
    u = _conv_silu(taps, cw_ref, cb_ref)
    gates = _gate_values(g_ref, gb_ref, n_heads)
    tri, eye, lastm = tri_ref[...] > 0.5, eye_ref[...] > 0.5, last_ref[...] > 0.5
    mo_ref[...] = jnp.zeros_like(mo_ref)

    def load_c(i, h):
        return c0_ref[i, h]

    def store_c(i, h, val):
        co_ref[i, h] = val

    def store_n(i, h, al_i, ksum):
        sl = slice(h * HEAD_DIM, (h + 1) * HEAD_DIM)
        no_ref[i:i + 1, sl] = al_i * n0_ref[i:i + 1, sl] + ksum

    def nrows(h):
        return nr_ref[:, h * HEAD_DIM:(h + 1) * HEAD_DIM]

    _mlstm_block(u, v_ref, o_ref, gates, m0_ref, mo_ref, nrows, tri, eye, lastm, gn_ref, y_ref,
                 load_c, store_c, store_n, rs_ref[...], n_heads, n_sub, seq_len)


def _seq_masks(n_sub, seq_len):
    rows = n_sub * seq_len
    r = np.arange(rows)
    seq, t = r // seq_len, r % seq_len
    same = seq[:, None] == seq[None, :]
    tri = (same & (t[None, :] <= t[:, None])).astype(np.float32)
    eye = np.eye(rows, dtype=np.float32)
    last = (r[None, :] == (seq[:, None] * seq_len + seq_len - 1)).astype(np.float32)
    return jnp.asarray(tri), jnp.asarray(eye), jnp.asarray(last)


def mlstm_prompt(proj, conv_w, conv_b, gate_b, gn, B, L, n_heads, gate_blk):
    W = n_heads * HEAD_DIM
    C = min(CHUNK, L)
    nc = L // C
    tri, eye, last = _seq_masks(1, C)
    kern = functools.partial(_mlstm_prompt_kernel, n_heads=n_heads)
    const2 = lambda shape: pl.BlockSpec(shape, lambda b, c: (0,) * len(shape))
    return pl.pallas_call(
        kern,
        out_shape=(jax.ShapeDtypeStruct((B * L, W), BF16),
                   jax.ShapeDtypeStruct((B, n_heads, HEAD_DIM, HEAD_DIM), F32),
                   jax.ShapeDtypeStruct((B, n_heads, 1, HEAD_DIM), F32),
                   jax.ShapeDtypeStruct((B, 8, HEAD_DIM), F32)),
        grid=(B, nc),
        in_specs=[pl.BlockSpec((C, 2 * W), lambda b, c: (b * nc + c, 2)),
                  pl.BlockSpec((C, W), lambda b, c: (b * nc + c, 6)),
                  pl.BlockSpec((C, W), lambda b, c: (b * nc + c, 7)),
                  pl.BlockSpec((C, HEAD_DIM), lambda b, c: (b * nc + c, gate_blk)),
                  const2((CONV_W, 2 * W)), const2((1, 2 * W)), const2((1, HEAD_DIM)), const2((1, W)),
                  const2(tri.shape), const2(eye.shape), const2(last.shape)],
        out_specs=(pl.BlockSpec((C, W), lambda b, c: (b * nc + c, 0)),
                   pl.BlockSpec((None, n_heads, HEAD_DIM, HEAD_DIM), lambda b, c: (b, 0, 0, 0)),
                   pl.BlockSpec((None, n_heads, 1, HEAD_DIM), lambda b, c: (b, 0, 0, 0)),
                   pl.BlockSpec((None, 8, HEAD_DIM), lambda b, c: (b, 0, 0))),
        scratch_shapes=[pltpu.VMEM((8 + C, 2 * W), F32),
                        pltpu.VMEM((n_heads, HEAD_DIM, HEAD_DIM), F32),
                        pltpu.VMEM((n_heads, 1, HEAD_DIM), F32),
                        pltpu.VMEM((C, HEAD_DIM), F32)],
        compiler_params=_params(("parallel", "arbitrary"), 32),
        name="mlstm_prompt",
    )(proj, proj, proj, proj, conv_w, conv_b, gate_b, gn, tri, eye, last)


def mlstm_sample(proj, conv_w, conv_b, gate_b, gn, conv_rows, c_state, n_state, n_rows, m_rows,
                 layer, row0, DB, T, n_heads, gate_blk, G):
    W = n_heads * HEAD_DIM
    R = G * T
    rb0 = row0 // R
    tri, eye, last = _seq_masks(G, T)
    ones = np.ones((1, HEAD_DIM), np.float32)
    rowseq = jnp.asarray(np.repeat(np.arange(G, dtype=np.float32), T)[:, None] * ones)
    rowt = jnp.asarray(np.tile(np.arange(T, dtype=np.float32), G)[:, None] * ones)
    kern = functools.partial(_mlstm_sample_kernel, n_heads=n_heads, n_sub=G, seq_len=T)
    const1 = lambda shape: pl.BlockSpec(shape, lambda g: (0,) * len(shape))
    return pl.pallas_call(
        kern,
        out_shape=(jax.ShapeDtypeStruct((DB * T, W), BF16),
                   jax.ShapeDtypeStruct((DB, n_heads, HEAD_DIM, HEAD_DIM), F32),
                   jax.ShapeDtypeStruct((DB, W), F32),
                   jax.ShapeDtypeStruct((DB * T, HEAD_DIM), F32)),
        grid=(DB // G,),
        in_specs=[pl.BlockSpec((R, 2 * W), lambda g: (rb0 + g, 2)),
                  pl.BlockSpec((R, W), lambda g: (rb0 + g, 6)),
                  pl.BlockSpec((R, W), lambda g: (rb0 + g, 7)),
                  pl.BlockSpec((R, HEAD_DIM), lambda g: (rb0 + g, gate_blk)),
                  const1((CONV_W, 2 * W)), const1((1, 2 * W)), const1((1, HEAD_DIM)), const1((1, W)),
                  const1(tri.shape), const1(eye.shape), const1(last.shape),
                  const1(rowseq.shape), const1(rowt.shape),
                  pl.BlockSpec((None, R, 2 * W), lambda g: (layer, g, 0)),
                  pl.BlockSpec((None, G, n_heads, HEAD_DIM, HEAD_DIM), lambda g: (layer, g, 0, 0, 0)),
                  pl.BlockSpec((None, G, W), lambda g: (layer, g, 0)),
                  pl.BlockSpec((None, R, W), lambda g: (layer, g, 0)),
                  pl.BlockSpec((None, R, HEAD_DIM), lambda g: (layer, g, 0))],
        out_specs=(pl.BlockSpec((R, W), lambda g: (g, 0)),
                   pl.BlockSpec((G, n_heads, HEAD_DIM, HEAD_DIM), lambda g: (g, 0, 0, 0)),
                   pl.BlockSpec((G, W), lambda g: (g, 0)),
                   pl.BlockSpec((R, HEAD_DIM), lambda g: (g, 0))),
        compiler_params=_params(("parallel",), 32),
        name="mlstm_sample",
    )(proj, proj, proj, proj, conv_w, conv_b, gate_b, gn, tri, eye, last, rowseq, rowt,
      conv_rows, c_state, n_state, n_rows, m_rows)


def _rope64x2(x, cos4, sin_lo, sin_hi):
    return x * cos4 + pltpu.roll(x, 96, 1) * sin_lo + pltpu.roll(x, 32, 1) * sin_hi


def _mla_prep_kernel(qn_ref, qp_ref, kv_ref, cos_ref, slo_ref, shi_ref, kvn_ref, wuk_ref,
                     kvr_ref, kb_ref, q_ref, *, n_heads, lat):
    cos4, slo, shi = cos_ref[...], slo_ref[...], shi_ref[...]
    c_new = _rms(kv_ref[:, 0:lat], kvn_ref[...])
    kr = _rope64x2(kv_ref[:, lat:lat + HEAD_DIM], cos4, slo, shi)
    kvr_ref[:, 0:lat] = c_new
    kvr_ref[:, lat:lat + HEAD_DIM] = kr
    kb_ref[:, 0:lat] = c_new.astype(BF16)
    kb_ref[:, lat:lat + HEAD_DIM] = (kr + pltpu.roll(kr, HEAD_DIM // 2, 1)).astype(BF16)
    qw = lat + HEAD_DIM
    lane = lax.broadcasted_iota(jnp.int32, (qp_ref.shape[0], HEAD_DIM), 1)
    for p in range(n_heads // 2):
        pe = _rope64x2(qp_ref[:, p * HEAD_DIM:(p + 1) * HEAD_DIM], cos4, slo, shi)
        for half in range(2):
            h = 2 * p + half
            q_lat = jnp.dot(qn_ref[:, h * HEAD_DIM:(h + 1) * HEAD_DIM].astype(BF16), wuk_ref[h],
                            preferred_element_type=F32)
            q_ref[:, h * qw:h * qw + lat] = q_lat.astype(BF16)
            keep = (lane < HEAD_DIM // 2) if half == 0 else (lane >= HEAD_DIM // 2)
            q_ref[:, h * qw + lat:(h + 1) * qw] = jnp.where(keep, pe, 0.0).astype(BF16)


def mla_prep(proj, cos4, sin_lo, sin_hi, kv_norm, w_uk_t, n_heads, lat):
    M = proj.shape[0]
    tm = _tile(M, 272, 16)
    qw = lat + HEAD_DIM
    kern = functools.partial(_mla_prep_kernel, n_heads=n_heads, lat=lat)
    row = lambda w, j: pl.BlockSpec((tm, w), lambda i, j=j: (i, j))
    const1 = lambda shape: pl.BlockSpec(shape, lambda i: (0,) * len(shape))
    nope_w, pe_w = n_heads * HEAD_DIM, n_heads * HEAD_DIM // 2
    return pl.pallas_call(
        kern,
        out_shape=(jax.ShapeDtypeStruct((M, qw), F32),
                   jax.ShapeDtypeStruct((M, qw), BF16),
                   jax.ShapeDtypeStruct((M, n_heads * qw), BF16)),
        grid=(M // tm,),
        in_specs=[row(nope_w, 4), row(pe_w, 10), row(qw, 15),
                  row(HEAD_DIM, 0), row(HEAD_DIM, 0), row(HEAD_DIM, 0),
                  const1((1, lat)), const1(w_uk_t.shape)],
        out_specs=(row(qw, 0), row(qw, 0), row(n_heads * qw, 0)),
        compiler_params=_params(("parallel",), 32),
        name="mla_prep",
    )(proj, proj, proj, cos4, sin_lo, sin_hi, kv_norm, w_uk_t)


def _mla_prompt_kernel(q_ref, kb_ref, o_ref, qs_ref, m_ref, l_ref, acc_ref, *, n_heads, lat, tq, scale):
    i = pl.program_id(1)
    qw = lat + HEAD_DIM
    for h in range(n_heads):
        qs_ref[h * tq:(h + 1) * tq, :] = q_ref[:, h * qw:(h + 1) * qw]
    m_ref[...] = jnp.full(m_ref.shape, NEG_INF, F32)
    l_ref[...] = jnp.zeros_like(l_ref)
    acc_ref[...] = jnp.zeros_like(acc_ref)
    rows = n_heads * tq
    qpos = i * tq + (lax.broadcasted_iota(jnp.int32, (rows, tq), 0) & (tq - 1))
    kloc = lax.broadcasted_iota(jnp.int32, (rows, tq), 1)

    def chunk(j, carry):
        kblk = kb_ref[pl.ds(pl.multiple_of(j * tq, tq), tq), :]
        s = lax.dot_general(qs_ref[...], kblk, NT_DIMS, preferred_element_type=F32) * scale
        s = jnp.where(j * tq + kloc <= qpos, s, NEG_INF)
        m_old = m_ref[...]
        m_new = jnp.maximum(m_old, jnp.max(s, axis=1, keepdims=True))
        alpha = jnp.exp(m_old - m_new)
        p = jnp.exp(s - m_new)
        l_ref[...] = alpha * l_ref[...] + jnp.sum(p, axis=1, keepdims=True)
        acc_ref[...] = alpha * acc_ref[...] + jnp.dot(p.astype(BF16), kblk[:, 0:lat], preferred_element_type=F32)
        m_ref[...] = m_new
        return carry

    lax.fori_loop(0, i + 1, chunk, 0)
    o = acc_ref[...] / l_ref[...]
    for h in range(n_heads):
        o_ref[:, h * lat:(h + 1) * lat] = o[h * tq:(h + 1) * tq, :].astype(BF16)


def mla_prompt(q, kb, B, L, n_heads, lat, scale):
    qw = lat + HEAD_DIM
    tq = _tile(L, 256, 128)
    assert tq & (tq - 1) == 0
    nq = L // tq
    kern = functools.partial(_mla_prompt_kernel, n_heads=n_heads, lat=lat, tq=tq, scale=scale)
    return pl.pallas_call(
        kern,
        out_shape=jax.ShapeDtypeStruct((B * L, n_heads * lat), BF16),
        grid=(B, nq),
        in_specs=[pl.BlockSpec((tq, n_heads * qw), lambda b, i: (b * nq + i, 0)),
                  pl.BlockSpec((L, qw), lambda b, i: (b, 0))],
        out_specs=pl.BlockSpec((tq, n_heads * lat), lambda b, i: (b * nq + i, 0)),
        scratch_shapes=[pltpu.VMEM((n_heads * tq, qw), BF16),
                        pltpu.VMEM((n_heads * tq, 1), F32),
                        pltpu.VMEM((n_heads * tq, 1), F32),
                        pltpu.VMEM((n_heads * tq, lat), F32)],
        compiler_params=_params(("parallel", "arbitrary"), 48),
        name="mla_prompt",
    )(q, kb)


def _mla_sample_kernel(pt_ref, q_ref, kn_ref, *refs, n_pages_step, lat, n_heads, seq_len, scale):
    P = n_pages_step
    kv_refs, kr_refs = refs[0:P], refs[P:2 * P]
    o_ref, cbuf_ref, rbuf_ref, m_ref, l_ref, acc_ref = refs[2 * P:]
    j = pl.program_id(1)
    page = kv_refs[0].shape[0]

    @pl.when(j == 0)
    def _():
        m_ref[...] = jnp.full(m_ref.shape, NEG_INF, F32)
        l_ref[...] = jnp.zeros_like(l_ref)
        acc_ref[...] = jnp.zeros_like(acc_ref)

    for p in range(P):
        cbuf_ref[p * page:(p + 1) * page, :] = kv_refs[p][...].astype(BF16)
        kr = kr_refs[p][...]
        rbuf_ref[p * page:(p + 1) * page, :] = jnp.concatenate([kr, kr], axis=1).astype(BF16)

    def update(s, vals):
        m_old = m_ref[...]
        m_new = jnp.maximum(m_old, jnp.max(s, axis=1, keepdims=True))
        alpha = jnp.exp(m_old - m_new)
        p = jnp.exp(s - m_new)
        l_ref[...] = alpha * l_ref[...] + jnp.sum(p, axis=1, keepdims=True)
        acc_ref[...] = alpha * acc_ref[...] + jnp.dot(p.astype(BF16), vals, preferred_element_type=F32)
        m_ref[...] = m_new

    q = q_ref[...]
    cb = cbuf_ref[...]
    s = (lax.dot_general(q[:, 0:lat], cb, NT_DIMS, preferred_element_type=F32)
         + lax.dot_general(q[:, lat:], rbuf_ref[...], NT_DIMS, preferred_element_type=F32)) * scale
    update(s, cb)

    @pl.when(j == pl.num_programs(1) - 1)
    def _():
        kn = kn_ref[...]
        sn = lax.dot_general(q, kn, NT_DIMS, preferred_element_type=F32) * scale
        t_q = jnp.right_shift(lax.broadcasted_iota(jnp.int32, sn.shape, 0), n_heads.bit_length() - 1)
        t_k = lax.broadcasted_iota(jnp.int32, sn.shape, 1)
        sn = jnp.where(t_k <= t_q, sn, NEG_INF)
        update(sn, kn[:, 0:lat])
        o_ref[...] = (acc_ref[...] / l_ref[...]).astype(BF16)


def mla_sample(q_s, kb_new, cache_kv, cache_kr, page_table, layer, n_heads, lat, scale):
    DB, rows, qw = q_s.shape
    n_pages = page_table.shape[1]
    page = cache_kv.shape[2]
    rope_w = cache_kr.shape[3]
    P = _tile(n_pages, 16, 1)
    seq_len = rows // n_heads
    assert seq_len <= page
    kern = functools.partial(_mla_sample_kernel, n_pages_step=P, lat=lat, n_heads=n_heads,
                             seq_len=seq_len, scale=scale)
    kv_specs = [pl.BlockSpec((None, None, page, lat),
                             lambda b, j, pt, p=p: (layer, pt[b * n_pages + j * P + p], 0, 0)) for p in range(P)]
    kr_specs = [pl.BlockSpec((None, None, page, rope_w),
                             lambda b, j, pt, p=p: (layer, pt[b * n_pages + j * P + p], 0, 0)) for p in range(P)]
    grid_spec = pltpu.PrefetchScalarGridSpec(
        num_scalar_prefetch=1,
        grid=(DB, n_pages // P),
        in_specs=[pl.BlockSpec((None, rows, qw), lambda b, j, pt: (b, 0, 0)),
                  pl.BlockSpec((None, page, qw), lambda b, j, pt: (b, 0, 0))] + kv_specs + kr_specs,
        out_specs=pl.BlockSpec((None, rows, lat), lambda b, j, pt: (b, 0, 0)),
        scratch_shapes=[pltpu.VMEM((P * page, lat), BF16),
                        pltpu.VMEM((P * page, 2 * rope_w), BF16),
                        pltpu.VMEM((rows, 1), F32),
                        pltpu.VMEM((rows, 1), F32),
                        pltpu.VMEM((rows, lat), F32)])
    return pl.pallas_call(
        kern,
        out_shape=jax.ShapeDtypeStruct((DB, rows, lat), BF16),
        grid_spec=grid_spec,
        compiler_params=_params(("parallel", "arbitrary"), 48),
        name="mla_sample",
    )(page_table.reshape(-1), q_s, kb_new, *([cache_kv] * P), *([cache_kr] * P))


def _rope_tables(pos, dim):
    inv = ROPE_BASE ** (-jnp.arange(0, dim, 2, dtype=F32) / dim)
    ang = pos.astype(F32)[:, None] * inv[None, :]
    return jnp.cos(ang), jnp.sin(ang)


def _relayout_w_in(w_in, n_mla, nope, rope_w, lat):
    depth, D, _ = w_in.shape
    main = 4 * (D // 4) + D
    n_gate = 2 * (D // (4 * HEAD_DIM))
    w = w_in.astype(BF16)
    gates = w[..., main:main + n_gate]
    q0 = main + n_gate
    mlq = w[..., q0:q0 + n_mla * (nope + rope_w)].reshape(depth, D, n_mla, nope + rope_w)
    q_nope = mlq[..., :nope].reshape(depth, D, n_mla * nope)
    q_pe = mlq[..., nope:].reshape(depth, D, n_mla * rope_w)
    k0 = q0 + n_mla * (nope + rope_w)
    mkv = w[..., k0:k0 + lat + rope_w]
    zeros = lambda n: jnp.zeros((depth, D, n), BF16)
    return jnp.concatenate([w[..., :main], q_nope, q_pe, gates, zeros(HEAD_DIM - n_gate),
                            mkv, zeros(HEAD_DIM - rope_w)], axis=-1)


def kernel(x_prompt, x_sample, cache_mla_kv, cache_mla_kr, page_table, state_ret, state_mlstm_C, state_mlstm_n, state_mlstm_m, state_mlstm_conv, norm_attn, norm_mlp, norm_final, w_in, ret_gn, mlstm_conv_w, mlstm_conv_b, mlstm_gate_b, mlstm_gn, mla_kv_norm, w_uk, w_uv, w_out, w_up, w_down):
    B, L, D = x_prompt.shape
    DB, T, _ = x_sample.shape
    depth = w_in.shape[0]
    n_heads = D // (4 * HEAD_DIM)
    W = n_heads * HEAD_DIM
    n_mla = D // (2 * HEAD_DIM)
    lat = mla_kv_norm.shape[1]
    rope_w = cache_mla_kr.shape[3]
    page = cache_mla_kv.shape[2]
    past_len = page_table.shape[1] * page
    assert D == 2048 and lat == 2 * HEAD_DIM and rope_w == HEAD_DIM // 2 and T >= CONV_W - 1 and L % min(CHUNK, L) == 0
    mla_scale = (HEAD_DIM + rope_w) ** -0.5
    G = 8
    n_p, n_s = B * L, DB * T
    gate_blk = (4 * W + 4 * W + n_mla * HEAD_DIM + n_mla * rope_w) // HEAD_DIM
    assert n_p % (G * T) == 0 and DB % G == 0

    pos = jnp.concatenate([jnp.tile(jnp.arange(L, dtype=jnp.int32), B),
                           jnp.tile(past_len + jnp.arange(T, dtype=jnp.int32), DB)])
    cos_h, sin_h = _rope_tables(pos, HEAD_DIM)
    cos2 = jnp.concatenate([cos_h, cos_h], axis=1)
    sin2 = jnp.concatenate([-sin_h, sin_h], axis=1)
    cos_r, sin_r = _rope_tables(pos, rope_w)
    zero_r = jnp.zeros_like(sin_r)
    cos4 = jnp.concatenate([cos_r] * 4, axis=1)
    sin_lo = jnp.concatenate([-sin_r, zero_r, -sin_r, zero_r], axis=1)
    sin_hi = jnp.concatenate([zero_r, sin_r, zero_r, sin_r], axis=1)

    w_in_b = _relayout_w_in(w_in, n_mla, HEAD_DIM, rope_w, lat)
    w_out_b, w_up_b, w_down_b = w_out.astype(BF16), w_up.astype(BF16), w_down.astype(BF16)
    w_uk_t = jnp.transpose(w_uk, (0, 2, 3, 1)).astype(BF16)
    w_uv_t = jnp.transpose(w_uv, (0, 2, 1, 3)).astype(BF16)
    gate_b = jnp.pad(mlstm_gate_b, ((0, 0), (0, HEAD_DIM - mlstm_gate_b.shape[1])))

    conv_rows = jnp.pad(state_mlstm_conv, ((0, 0), (0, 0), (T - (CONV_W - 1), 0), (0, 0))).reshape(depth, n_s, 2 * W)
    n_state = state_mlstm_n.reshape(depth, DB, W)
    n_rows = jnp.repeat(n_state, T, axis=1)
    m_rows = jnp.repeat(jnp.pad(state_mlstm_m, ((0, 0), (0, 0), (0, HEAD_DIM - n_heads))), T, axis=1)

    h = jnp.concatenate([x_prompt.reshape(n_p, D), x_sample.reshape(n_s, D)], axis=0)
    outs = [[] for _ in range(14)]
    for l in range(depth):
        proj = norm_matmul(h, norm_attn[l][None], w_in_b[l])

        gn_r = ret_gn[l][None]
        yr_p, ret_p = retention_prompt(proj, cos2, sin2, gn_r, B, L, n_heads)
        yr_s, ret_s = retention_sample(proj, cos2, sin2, gn_r, state_ret, l, n_p, DB, T, n_heads, G)

        cw, cb, gb, gn_m = mlstm_conv_w[l], mlstm_conv_b[l][None], gate_b[l][None], mlstm_gn[l][None]
        ym_p, c_p, nn_p, mm_p = mlstm_prompt(proj, cw, cb, gb, gn_m, B, L, n_heads, gate_blk)
        ym_s, c_s, nn_s, mm_s = mlstm_sample(proj, cw, cb, gb, gn_m, conv_rows, state_mlstm_C, n_state,
                                             n_rows, m_rows, l, n_p, DB, T, n_heads, gate_blk, G)

        kvr, kb, q = mla_prep(proj, cos4, sin_lo, sin_hi, mla_kv_norm[l][None], w_uk_t[l], n_mla, lat)
        o_p = mla_prompt(q, kb, B, L, n_mla, lat, mla_scale)
        q_s = q[n_p:].reshape(DB, T * n_mla, lat + HEAD_DIM)
        kb_new = jnp.pad(kb[n_p:].reshape(DB, T, lat + HEAD_DIM), ((0, 0), (0, page - T), (0, 0)))
        o_s = mla_sample(q_s, kb_new, cache_mla_kv, cache_mla_kr, page_table, l, n_mla, lat, mla_scale)

        y_ret = jnp.concatenate([yr_p, yr_s], axis=0)
        y_m = jnp.concatenate([ym_p, ym_s], axis=0)
        o_lat = jnp.concatenate([o_p, o_s.reshape(n_s, n_mla * lat)], axis=0)
        h = outproj_residual(h, y_ret, y_m, o_lat, w_uv_t[l], w_out_b[l])
        h = mlp_residual(h, norm_mlp[l][None], w_up_b[l], w_down_b[l])

        mqk = proj[:, 4 * W:6 * W]
        per_layer = (
            kvr[:n_p, :lat].reshape(B, L, lat), kvr[:n_p, lat:lat + rope_w].reshape(B, L, rope_w),
            kvr[n_p:, :lat].reshape(DB, T, lat), kvr[n_p:, lat:lat + rope_w].reshape(DB, T, rope_w),
            ret_p, ret_s, c_p, c_s,
            nn_p.reshape(B, n_heads, HEAD_DIM), nn_s.reshape(DB, n_heads, HEAD_DIM),
            mm_p[:, 0, :n_heads], mm_s.reshape(DB, T, HEAD_DIM)[:, T - 1, :n_heads],
            mqk[:n_p].reshape(B, L, 2 * W)[:, L - (CONV_W - 1):],
            mqk[n_p:].reshape(DB, T, 2 * W)[:, T - (CONV_W - 1):],
        )
        for acc, val in zip(outs, per_layer):
            acc.append(val)

    y = final_norm(h, norm_final[None])
    return (y[:n_p].reshape(B, L, D), y[n_p:].reshape(DB, T, D)) + tuple(jnp.stack(o) for o in outs)
```

```python
import functools

import numpy as np
import jax
import jax.numpy as jnp
from jax import lax
from jax.experimental import pallas as pl
from jax.experimental.pallas import tpu as pltpu

F32 = jnp.float32
BF16 = jnp.bfloat16

HEAD_DIM = 128
CONV_W = 4
CHUNK = 128
ROPE_BASE = 10000.0
EPS = 1e-6
QK_SCALE = HEAD_DIM ** -0.5
NEG_INF = float("-inf")

MAX_PAGES_PER_STEP = 64

NT_DIMS = (((1,), (1,)), ((), ()))
TN_DIMS = (((0,), (0,)), ((), ()))


def _params(semantics, vmem_mb):
    return pltpu.CompilerParams(dimension_semantics=semantics, vmem_limit_bytes=vmem_mb << 20)


def _tile(n, target, mult):
    best = None
    for t in range(mult, min(n, target) + 1, mult):
        if n % t == 0:
            best = t
    assert best is not None, (n, target, mult)
    return best


def _rms(x, g):
    ms = jnp.mean(x * x, axis=-1, keepdims=True)
    return (x * lax.rsqrt(ms + EPS)) * g


def _head_ln(y, g):
    mu = jnp.mean(y, axis=-1, keepdims=True)
    yc = y - mu
    var = jnp.mean(yc * yc, axis=-1, keepdims=True)
    return (yc * lax.rsqrt(var + EPS)) * g


def _norm_matmul_kernel(x_ref, g_ref, w_ref, o_ref, xn_ref):
    @pl.when(pl.program_id(1) == 0)
    def _():
        xn_ref[...] = _rms(x_ref[...], g_ref[...]).astype(BF16)

    o_ref[...] = jnp.dot(xn_ref[...], w_ref[...], preferred_element_type=F32)


def norm_matmul(x, g, w, layer):
    M, D = x.shape
    N = w.shape[2]
    tm = _tile(M, 1088, 16)
    tn = _tile(N, 512, 128)
    return pl.pallas_call(
        _norm_matmul_kernel,
        out_shape=jax.ShapeDtypeStruct((M, N), F32),
        grid=(M // tm, N // tn),
        in_specs=[
            pl.BlockSpec((tm, D), lambda i, j: (i, 0)),
            pl.BlockSpec((1, D), lambda i, j: (0, 0)),
            pl.BlockSpec((None, D, tn), lambda i, j: (layer, 0, j)),
        ],
        out_specs=pl.BlockSpec((tm, tn), lambda i, j: (i, j)),
        scratch_shapes=[pltpu.VMEM((tm, D), BF16)],
        compiler_params=_params(("parallel", "arbitrary"), 48),
        name="norm_inproj",
    )(x, g, w)


def _mlp_kernel(x_ref, g_ref, wu_ref, wd_ref, o_ref, xn_ref):
    @pl.when(pl.program_id(1) == 0)
    def _():
        x = x_ref[...]
        xn_ref[...] = _rms(x, g_ref[...]).astype(BF16)
        o_ref[...] = x

    u = jnp.dot(xn_ref[...], wu_ref[...], preferred_element_type=F32)
    a = jnp.maximum(u, 0.0)
    o_ref[...] += jnp.dot((a * a).astype(BF16), wd_ref[...], preferred_element_type=F32)


def mlp_residual(x, g, w_up, w_down, layer):
    M, D = x.shape
    F = w_up.shape[2]
    tm = _tile(M, 1088, 16)
    tf = _tile(F, 512, 128)
    return pl.pallas_call(
        _mlp_kernel,
        out_shape=jax.ShapeDtypeStruct((M, D), F32),
        grid=(M // tm, F // tf),
        in_specs=[
            pl.BlockSpec((tm, D), lambda i, f: (i, 0), pipeline_mode=pl.Buffered(1)),
            pl.BlockSpec((1, D), lambda i, f: (0, 0)),
            pl.BlockSpec((None, D, tf), lambda i, f: (layer, 0, f)),
            pl.BlockSpec((None, tf, D), lambda i, f: (layer, f, 0)),
        ],
        out_specs=pl.BlockSpec((tm, D), lambda i, f: (i, 0)),
        scratch_shapes=[pltpu.VMEM((tm, D), BF16)],
        compiler_params=_params(("parallel", "arbitrary"), 56),
        name="mlp",
    )(x, g, w_up, w_down)


def _outproj_kernel(x_ref, yr_ref, ym_ref, o_ref, wuv_ref, w_ref, out_ref, ymla_ref, *, n_mla, w_ret, w_m):
    lat = wuv_ref.shape[1]
    for h in range(n_mla):
        yh = jnp.dot(o_ref[:, h * lat:(h + 1) * lat], wuv_ref[h], preferred_element_type=F32)
        ymla_ref[:, h * HEAD_DIM:(h + 1) * HEAD_DIM] = yh.astype(BF16)
    acc = x_ref[...]
    acc += jnp.dot(yr_ref[...], w_ref[0:w_ret, :], preferred_element_type=F32)
    acc += jnp.dot(ym_ref[...], w_ref[w_ret:w_ret + w_m, :], preferred_element_type=F32)
    acc += jnp.dot(ymla_ref[...], w_ref[w_ret + w_m:, :], preferred_element_type=F32)
    out_ref[...] = acc


def outproj_residual(x, y_ret, y_m, o_lat, w_uv, w_out, layer):
    M, D = x.shape
    _, n_mla, lat, _ = w_uv.shape
    w_ret, w_m = y_ret.shape[1], y_m.shape[1]
    tm = _tile(M, 272, 16)
    kern = functools.partial(_outproj_kernel, n_mla=n_mla, w_ret=w_ret, w_m=w_m)
    return pl.pallas_call(
        kern,
        out_shape=jax.ShapeDtypeStruct((M, D), F32),
        grid=(M // tm,),
        in_specs=[
            pl.BlockSpec((tm, D), lambda i: (i, 0)),
            pl.BlockSpec((tm, w_ret), lambda i: (i, 0)),
            pl.BlockSpec((tm, w_m), lambda i: (i, 0)),
            pl.BlockSpec((tm, n_mla * lat), lambda i: (i, 0)),
            pl.BlockSpec((None,) + w_uv.shape[1:], lambda i: (layer, 0, 0, 0)),
            pl.BlockSpec((None,) + w_out.shape[1:], lambda i: (layer, 0, 0)),
        ],
        out_specs=pl.BlockSpec((tm, D), lambda i: (i, 0)),
        scratch_shapes=[pltpu.VMEM((tm, n_mla * HEAD_DIM), BF16)],
        compiler_params=_params(("parallel",), 48),
        name="outproj",
    )(x, y_ret, y_m, o_lat, w_uv, w_out)


def _final_norm_kernel(x_ref, g_ref, o_ref):
    o_ref[...] = _rms(x_ref[...], g_ref[...])


def final_norm(x, g):
    M, D = x.shape
    tm = _tile(M, 544, 8)
    return pl.pallas_call(
        _final_norm_kernel,
        out_shape=jax.ShapeDtypeStruct((M, D), F32),
        grid=(M // tm,),
        in_specs=[pl.BlockSpec((tm, D), lambda i: (i, 0)), pl.BlockSpec((1, D), lambda i: (0, 0))],
        out_specs=pl.BlockSpec((tm, D), lambda i: (i, 0)),
        compiler_params=_params(("parallel",), 32),
        name="final_norm",
    )(x, g)


def _rope128(x, cos2, sin2):
    return x * cos2 + pltpu.roll(x, HEAD_DIM // 2, 1) * sin2


def _ret_block(q_ref, k_ref, v_ref, g_ref, cos_ref, sin_ref, dm_ref, qd_ref, kd_ref, gn_ref, y_ref,
               load_state, store_state, rowseq, n_heads, n_sub, cdec):
    cos2, sin2 = cos_ref[...], sin_ref[...]
    for h in range(n_heads):
        sl = slice(h * HEAD_DIM, (h + 1) * HEAD_DIM)
        qr = _rope128(q_ref[:, sl], cos2, sin2)
        kr = _rope128(k_ref[:, sl], cos2, sin2) * QK_SCALE
        vb = v_ref[:, sl].astype(BF16)
        a = lax.dot_general(qr.astype(BF16), kr.astype(BF16), NT_DIMS, preferred_element_type=F32) * dm_ref[h]
        o = jnp.dot(a.astype(BF16), vb, preferred_element_type=F32)
        qs = (qr * qd_ref[:, sl]).astype(BF16)
        ks = kr * kd_ref[:, sl]
        for i in range(n_sub):
            s_old = load_state(i, h)
            inter = jnp.dot(qs, s_old.astype(BF16), preferred_element_type=F32)
            if n_sub == 1:
                o = o + inter
                ks_i = ks
            else:
                sel = rowseq == float(i)
                o = o + jnp.where(sel, inter, 0.0)
                ks_i = jnp.where(sel, ks, 0.0)
            kv = lax.dot_general(ks_i.astype(BF16), vb, TN_DIMS, preferred_element_type=F32)
            store_state(i, h, s_old * cdec[h] + kv)
        yn = _head_ln(o, gn_ref[:, sl])
        y_ref[:, sl] = (jax.nn.silu(g_ref[:, sl]) * yn).astype(BF16)


def _ret_prompt_kernel(q_ref, k_ref, v_ref, g_ref, cos_ref, sin_ref, dm_ref, qd_ref, kd_ref, gn_ref,
                       y_ref, so_ref, s_ref, *, n_heads, cdec):
    c = pl.program_id(1)

    @pl.when(c == 0)
    def _():
        s_ref[...] = jnp.zeros_like(s_ref)

    def load_state(i, h):
        return s_ref[h]

    def store_state(i, h, val):
        s_ref[h] = val

    _ret_block(q_ref, k_ref, v_ref, g_ref, cos_ref, sin_ref, dm_ref, qd_ref, kd_ref, gn_ref, y_ref,
               load_state, store_state, None, n_heads, 1, cdec)

    @pl.when(c == pl.num_programs(1) - 1)
    def _():
        so_ref[...] = s_ref[...]


def _ret_sample_kernel(q_ref, k_ref, v_ref, g_ref, cos_ref, sin_ref, dm_ref, qd_ref, kd_ref, gn_ref,
                       rs_ref, s0_ref, y_ref, so_ref, *, n_heads, n_sub, cdec):
    def load_state(i, h):
        return s0_ref[i, h]

    def store_state(i, h, val):
        so_ref[i, h] = val

    _ret_block(q_ref, k_ref, v_ref, g_ref, cos_ref, sin_ref, dm_ref, qd_ref, kd_ref, gn_ref, y_ref,
               load_state, store_state, rs_ref[...], n_heads, n_sub, cdec)


def _ret_decay_tables(n_heads, chunk, n_sub):
    lg = np.log1p(-np.exp2(-5.0 - np.arange(n_heads, dtype=np.float64)))
    idx = np.arange(chunk, dtype=np.float64)
    diff = idx[:, None] - idx[None, :]
    dm1 = np.where(diff[None] >= 0, np.exp(np.maximum(diff, 0.0)[None] * lg[:, None, None]), 0.0)
    rows = n_sub * chunk
    dm = np.zeros((n_heads, rows, rows))
    for i in range(n_sub):
        dm[:, i * chunk:(i + 1) * chunk, i * chunk:(i + 1) * chunk] = dm1
    q_dec = np.exp((idx[:, None] + 1.0) * lg[None, :])
    k_dec = np.exp((chunk - 1.0 - idx)[:, None] * lg[None, :])
    qd = np.tile(np.repeat(q_dec, HEAD_DIM, axis=1), (n_sub, 1))
    kd = np.tile(np.repeat(k_dec, HEAD_DIM, axis=1), (n_sub, 1))
    cdec = tuple(float(np.float32(v)) for v in np.exp(chunk * lg))
    return jnp.asarray(dm, F32), jnp.asarray(qd, F32), jnp.asarray(kd, F32), cdec


def retention_prompt(proj, cos2, sin2, gn, B, L, n_heads):
    W = n_heads * HEAD_DIM
    C = min(CHUNK, L)
    nc = L // C
    dm, qd, kd, cdec = _ret_decay_tables(n_heads, C, 1)
    kern = functools.partial(_ret_prompt_kernel, n_heads=n_heads, cdec=cdec)
    col = lambda j: pl.BlockSpec((C, W), lambda b, c, j=j: (b * nc + c, j))
    const2 = lambda shape: pl.BlockSpec(shape, lambda b, c: (0,) * len(shape))
    return pl.pallas_call(
        kern,
        out_shape=(jax.ShapeDtypeStruct((B * L, W), BF16),
                   jax.ShapeDtypeStruct((B, n_heads, HEAD_DIM, HEAD_DIM), F32)),
        grid=(B, nc),
        in_specs=[col(0), col(1), col(2), col(3),
                  pl.BlockSpec((C, HEAD_DIM), lambda b, c: (b * nc + c, 0)),
                  pl.BlockSpec((C, HEAD_DIM), lambda b, c: (b * nc + c, 0)),
                  const2(dm.shape), const2(qd.shape), const2(kd.shape), const2((1, W))],
        out_specs=(pl.BlockSpec((C, W), lambda b, c: (b * nc + c, 0)),
                   pl.BlockSpec((None, n_heads, HEAD_DIM, HEAD_DIM), lambda b, c: (b, 0, 0, 0))),
        scratch_shapes=[pltpu.VMEM((n_heads, HEAD_DIM, HEAD_DIM), F32)],
        compiler_params=_params(("parallel", "arbitrary"), 32),
        name="retention_prompt",
    )(proj, proj, proj, proj, cos2, sin2, dm, qd, kd, gn)


def retention_sample(proj, cos2, sin2, gn, state, layer, row0, DB, T, n_heads, G):
    W = n_heads * HEAD_DIM
    R = G * T
    rb0 = row0 // R
    dm, qd, kd, cdec = _ret_decay_tables(n_heads, T, G)
    rowseq = jnp.asarray(np.repeat(np.arange(G, dtype=np.float32), T)[:, None] * np.ones((1, HEAD_DIM), np.float32))
    kern = functools.partial(_ret_sample_kernel, n_heads=n_heads, n_sub=G, cdec=cdec)
    col = lambda j: pl.BlockSpec((R, W), lambda g, j=j: (rb0 + g, j))
    const1 = lambda shape: pl.BlockSpec(shape, lambda g: (0,) * len(shape))
    return pl.pallas_call(
        kern,
        out_shape=(jax.ShapeDtypeStruct((DB * T, W), BF16),
                   jax.ShapeDtypeStruct((DB, n_heads, HEAD_DIM, HEAD_DIM), F32)),
        grid=(DB // G,),
        in_specs=[col(0), col(1), col(2), col(3),
                  pl.BlockSpec((R, HEAD_DIM), lambda g: (rb0 + g, 0)),
                  pl.BlockSpec((R, HEAD_DIM), lambda g: (rb0 + g, 0)),
                  const1(dm.shape), const1(qd.shape), const1(kd.shape), const1((1, W)),
                  const1(rowseq.shape),
                  pl.BlockSpec((None, G, n_heads, HEAD_DIM, HEAD_DIM), lambda g: (layer, g, 0, 0, 0))],
        out_specs=(pl.BlockSpec((R, W), lambda g: (g, 0)),
                   pl.BlockSpec((G, n_heads, HEAD_DIM, HEAD_DIM), lambda g: (g, 0, 0, 0))),
        compiler_params=_params(("parallel",), 32),
        name="retention_sample",
    )(proj, proj, proj, proj, cos2, sin2, dm, qd, kd, gn, rowseq, state)


def _mlstm_block(u, v_ref, o_ref, gates, mprev_ref, mout_ref, nrows, tri, eye, lastm, gn_ref, y_ref,
                 load_c, store_c, store_n, rowseq, n_heads, n_sub, seq_len):
    W = n_heads * HEAD_DIM
    for h in range(n_heads):
        sl = slice(h * HEAD_DIM, (h + 1) * HEAD_DIM)
        q = u[:, h * HEAD_DIM:(h + 1) * HEAD_DIM]
        k = u[:, W + h * HEAD_DIM:W + (h + 1) * HEAD_DIM] * QK_SCALE
        qb, kb = q.astype(BF16), k.astype(BF16)
        vb = v_ref[:, sl].astype(BF16)
        ig_col = gates[:, h:h + 1]
        lf_col = gates[:, n_heads + h:n_heads + h + 1]
        m_col = mprev_ref[:, h:h + 1]
        f_row = jnp.sum(jnp.where(eye, lf_col, 0.0), axis=0, keepdims=True)
        bt_col = jnp.sum(jnp.where(tri, f_row, 0.0), axis=1, keepdims=True)
        bt_row = jnp.sum(jnp.where(eye, bt_col, 0.0), axis=0, keepdims=True)
        it_row = jnp.sum(jnp.where(eye, ig_col, 0.0), axis=0, keepdims=True)
        logw = jnp.where(tri, bt_col - bt_row + it_row, NEG_INF)
        log_inter = bt_col + m_col
        m_i = jnp.maximum(log_inter, jnp.max(logw, axis=1, keepdims=True))
        w = jnp.exp(logw - m_i)
        a = jnp.exp(log_inter - m_i)
        s = lax.dot_general(qb, kb, NT_DIMS, preferred_element_type=F32) * w
        num = jnp.dot(s.astype(BF16), vb, preferred_element_type=F32)
        den = jnp.sum(s, axis=1, keepdims=True) + a * jnp.sum(q * nrows(h), axis=1, keepdims=True)
        m_row = jnp.sum(jnp.where(eye, m_i, 0.0), axis=0, keepdims=True)
        m_last = jnp.sum(jnp.where(lastm, m_row, 0.0), axis=1, keepdims=True)
        bt_last = jnp.sum(jnp.where(lastm, bt_row, 0.0), axis=1, keepdims=True)
        wl = jnp.exp(bt_last - bt_col + ig_col - m_last)
        al = jnp.exp(bt_last + m_col - m_last)
        kw = k * wl
        inter_sum = None
        for i in range(n_sub):
            c_old = load_c(i, h)
            inter = jnp.dot(qb, c_old.astype(BF16), preferred_element_type=F32)
            if n_sub == 1:
                kw_i = kw
            else:
                sel = rowseq == float(i)
                inter = jnp.where(sel, inter, 0.0)
                kw_i = jnp.where(sel, kw, 0.0)
            inter_sum = inter if inter_sum is None else inter_sum + inter
            al_i = al[i * seq_len:i * seq_len + 1, :]
            kv = lax.dot_general(kw_i.astype(BF16), vb, TN_DIMS, preferred_element_type=F32)
            store_c(i, h, al_i * c_old + kv)
            store_n(i, h, al_i, jnp.sum(kw_i, axis=0, keepdims=True))
        num = num + a * inter_sum
        hh = num / jnp.maximum(jnp.abs(den), jnp.exp(-m_i))
        yn = _head_ln(hh, gn_ref[:, sl])
        y_ref[:, sl] = (jax.nn.sigmoid(o_ref[:, sl]) * yn).astype(BF16)
        mout_ref[:, h:h + 1] = m_last


def _gate_values(g_ref, gb_ref, n_heads):
    pre = g_ref[...] + gb_ref[...]
    lane = lax.broadcasted_iota(jnp.int32, pre.shape, 1)
    lf = jnp.minimum(pre, 0.0) - jnp.log1p(jnp.exp(-jnp.abs(pre)))
    return jnp.where(lane < n_heads, pre, lf)


def _conv_silu(taps, cw_ref, cb_ref):
    acc = cb_ref[...]
    for j in range(CONV_W):
        acc = acc + taps[j] * cw_ref[j:j + 1, :]
    return jax.nn.silu(acc)


def _mlstm_prompt_kernel(x_ref, v_ref, o_ref, g_ref, cw_ref, cb_ref, gb_ref, gn_ref, tri_ref, eye_ref,
                         last_ref, y_ref, co_ref, no_ref, mo_ref,
                         xbuf_ref, c_ref, n_ref, m_ref, *, n_heads):
    c = pl.program_id(1)
    C = x_ref.shape[0]

    @pl.when(c == 0)
    def _():
        xbuf_ref[0:8, :] = jnp.zeros((8, xbuf_ref.shape[1]), F32)
        c_ref[...] = jnp.zeros_like(c_ref)
        n_ref[...] = jnp.zeros_like(n_ref)
        m_ref[...] = jnp.zeros_like(m_ref)

    xbuf_ref[8:8 + C, :] = x_ref[...]
    taps = [xbuf_ref[8 - (CONV_W - 1) + j:8 - (CONV_W - 1) + j + C, :] for j in range(CONV_W)]
    u = _conv_silu(taps, cw_ref, cb_ref)
    xbuf_ref[0:8, :] = x_ref[C - 8:C, :]
    gates = _gate_values(g_ref, gb_ref, n_heads)
    tri, eye, lastm = tri_ref[...] > 0.5, eye_ref[...] > 0.5, last_ref[...] > 0.5

    def load_c(i, h):
        return c_ref[h]

    def store_c(i, h, val):
        c_ref[h] = val

    def store_n(i, h, al_i, ksum):
        n_ref[h] = al_i * n_ref[h] + ksum

    def nrows(h):
        return n_ref[h]

    _mlstm_block(u, v_ref, o_ref, gates, m_ref, m_ref, nrows, tri, eye, lastm, gn_ref, y_ref,
                 load_c, store_c, store_n, None, n_heads, 1, C)

    @pl.when(c == pl.num_programs(1) - 1)
    def _():
        co_ref[...] = c_ref[...]
        no_ref[...] = n_ref[...]
        mo_ref[...] = m_ref[0:8, :]


def _mlstm_sample_kernel(x_ref, v_ref, o_ref, g_ref, cw_ref, cb_ref, gb_ref, gn_ref, tri_ref, eye_ref,
                         last_ref, rs_ref, rt_ref, st_ref, c0_ref, n0_ref, nr_ref, m0_ref,
                         y_ref, co_ref, no_ref, mo_ref, *, n_heads, n_sub, seq_len):
    R = x_ref.shape[0]
    x = x_ref[...]
    st = st_ref[...]
    tpos = rt_ref[:, 0:1]
    taps = []
    for j in range(CONV_W):
        s = CONV_W - 1 - j
        if s == 0:
            taps.append(x)
        else:
            cur = pltpu.roll(x, s, 0)
            old = pltpu.roll(st, (s - seq_len) % R, 0)
            taps.append(jnp.where(tpos >= float(s), cur, old))
    u = _conv_silu(taps, cw_ref, cb_ref)
    gates = _gate_values(g_ref, gb_ref, n_heads)
    tri, eye, lastm = tri_ref[...] > 0.5, eye_ref[...] > 0.5, last_ref[...] > 0.5
    mo_ref[...] = jnp.zeros_like(mo_ref)

    def load_c(i, h):
        return c0_ref[i, h]

    def store_c(i, h, val):
        co_ref[i, h] = val

    def store_n(i, h, al_i, ksum):
        sl = slice(h * HEAD_DIM, (h + 1) * HEAD_DIM)
        no_ref[i:i + 1, sl] = al_i * n0_ref[i:i + 1, sl] + ksum

    def nrows(h):
        return nr_ref[:, h * HEAD_DIM:(h + 1) * HEAD_DIM]

    _mlstm_block(u, v_ref, o_ref, gates, m0_ref, mo_ref, nrows, tri, eye, lastm, gn_ref, y_ref,
                 load_c, store_c, store_n, rs_ref[...], n_heads, n_sub, seq_len)


def _seq_masks(n_sub, seq_len):
    rows = n_sub * seq_len
    r = np.arange(rows)
    seq, t = r // seq_len, r % seq_len
    same = seq[:, None] == seq[None, :]
    tri = (same & (t[None, :] <= t[:, None])).astype(np.float32)
    eye = np.eye(rows, dtype=np.float32)
    last = (r[None, :] == (seq[:, None] * seq_len + seq_len - 1)).astype(np.float32)
    return jnp.asarray(tri), jnp.asarray(eye), jnp.asarray(last)


def mlstm_prompt(proj, conv_w, conv_b, gate_b, gn, B, L, n_heads, gate_blk):
    W = n_heads * HEAD_DIM
    C = min(CHUNK, L)
    nc = L // C
    tri, eye, last = _seq_masks(1, C)
    kern = functools.partial(_mlstm_prompt_kernel, n_heads=n_heads)
    const2 = lambda shape: pl.BlockSpec(shape, lambda b, c: (0,) * len(shape))
    return pl.pallas_call(
        kern,
        out_shape=(jax.ShapeDtypeStruct((B * L, W), BF16),
                   jax.ShapeDtypeStruct((B, n_heads, HEAD_DIM, HEAD_DIM), F32),
                   jax.ShapeDtypeStruct((B, n_heads, 1, HEAD_DIM), F32),
                   jax.ShapeDtypeStruct((B, 8, HEAD_DIM), F32)),
        grid=(B, nc),
        in_specs=[pl.BlockSpec((C, 2 * W), lambda b, c: (b * nc + c, 2)),
                  pl.BlockSpec((C, W), lambda b, c: (b * nc + c, 6)),
                  pl.BlockSpec((C, W), lambda b, c: (b * nc + c, 7)),
                  pl.BlockSpec((C, HEAD_DIM), lambda b, c: (b * nc + c, gate_blk)),
                  const2((CONV_W, 2 * W)), const2((1, 2 * W)), const2((1, HEAD_DIM)), const2((1, W)),
                  const2(tri.shape), const2(eye.shape), const2(last.shape)],
        out_specs=(pl.BlockSpec((C, W), lambda b, c: (b * nc + c, 0)),
                   pl.BlockSpec((None, n_heads, HEAD_DIM, HEAD_DIM), lambda b, c: (b, 0, 0, 0)),
                   pl.BlockSpec((None, n_heads, 1, HEAD_DIM), lambda b, c: (b, 0, 0, 0)),
                   pl.BlockSpec((None, 8, HEAD_DIM), lambda b, c: (b, 0, 0))),
        scratch_shapes=[pltpu.VMEM((8 + C, 2 * W), F32),
                        pltpu.VMEM((n_heads, HEAD_DIM, HEAD_DIM), F32),
                        pltpu.VMEM((n_heads, 1, HEAD_DIM), F32),
                        pltpu.VMEM((C, HEAD_DIM), F32)],
        compiler_params=_params(("parallel", "arbitrary"), 32),
        name="mlstm_prompt",
    )(proj, proj, proj, proj, conv_w, conv_b, gate_b, gn, tri, eye, last)


def mlstm_sample(proj, conv_w, conv_b, gate_b, gn, conv_rows, c_state, n_state, n_rows, m_rows,
                 layer, row0, DB, T, n_heads, gate_blk, G):
    W = n_heads * HEAD_DIM
    R = G * T
    rb0 = row0 // R
    tri, eye, last = _seq_masks(G, T)
    ones = np.ones((1, HEAD_DIM), np.float32)
    rowseq = jnp.asarray(np.repeat(np.arange(G, dtype=np.float32), T)[:, None] * ones)
    rowt = jnp.asarray(np.tile(np.arange(T, dtype=np.float32), G)[:, None] * ones)
    kern = functools.partial(_mlstm_sample_kernel, n_heads=n_heads, n_sub=G, seq_len=T)
    const1 = lambda shape: pl.BlockSpec(shape, lambda g: (0,) * len(shape))
    return pl.pallas_call(
        kern,
        out_shape=(jax.ShapeDtypeStruct((DB * T, W), BF16),
                   jax.ShapeDtypeStruct((DB, n_heads, HEAD_DIM, HEAD_DIM), F32),
                   jax.ShapeDtypeStruct((DB, W), F32),
                   jax.ShapeDtypeStruct((DB * T, HEAD_DIM), F32)),
        grid=(DB // G,),
        in_specs=[pl.BlockSpec((R, 2 * W), lambda g: (rb0 + g, 2)),
                  pl.BlockSpec((R, W), lambda g: (rb0 + g, 6)),
                  pl.BlockSpec((R, W), lambda g: (rb0 + g, 7)),
                  pl.BlockSpec((R, HEAD_DIM), lambda g: (rb0 + g, gate_blk)),
                  const1((CONV_W, 2 * W)), const1((1, 2 * W)), const1((1, HEAD_DIM)), const1((1, W)),
                  const1(tri.shape), const1(eye.shape), const1(last.shape),
                  const1(rowseq.shape), const1(rowt.shape),
                  pl.BlockSpec((None, R, 2 * W), lambda g: (layer, g, 0)),
                  pl.BlockSpec((None, G, n_heads, HEAD_DIM, HEAD_DIM), lambda g: (layer, g, 0, 0, 0)),
                  pl.BlockSpec((None, G, W), lambda g: (layer, g, 0)),
                  pl.BlockSpec((None, R, W), lambda g: (layer, g, 0)),
                  pl.BlockSpec((None, R, HEAD_DIM), lambda g: (layer, g, 0))],
        out_specs=(pl.BlockSpec((R, W), lambda g: (g, 0)),
                   pl.BlockSpec((G, n_heads, HEAD_DIM, HEAD_DIM), lambda g: (g, 0, 0, 0)),
                   pl.BlockSpec((G, W), lambda g: (g, 0)),
                   pl.BlockSpec((R, HEAD_DIM), lambda g: (g, 0))),
        compiler_params=_params(("parallel",), 32),
        name="mlstm_sample",
    )(proj, proj, proj, proj, conv_w, conv_b, gate_b, gn, tri, eye, last, rowseq, rowt,
      conv_rows, c_state, n_state, n_rows, m_rows)


def _rope64x2(x, cos4, sin_lo, sin_hi):
    return x * cos4 + pltpu.roll(x, 96, 1) * sin_lo + pltpu.roll(x, 32, 1) * sin_hi


def _mla_prep_kernel(qn_ref, qp_ref, kv_ref, cos_ref, slo_ref, shi_ref, kvn_ref, wuk_ref,
                     kvr_ref, kb_ref, q_ref, *, n_heads, lat):
    cos4, slo, shi = cos_ref[...], slo_ref[...], shi_ref[...]
    c_new = _rms(kv_ref[:, 0:lat], kvn_ref[...])
    kr = _rope64x2(kv_ref[:, lat:lat + HEAD_DIM], cos4, slo, shi)
    kvr_ref[:, 0:lat] = c_new
    kvr_ref[:, lat:lat + HEAD_DIM] = kr
    kb_ref[:, 0:lat] = c_new.astype(BF16)
    kb_ref[:, lat:lat + HEAD_DIM] = (kr + pltpu.roll(kr, HEAD_DIM // 2, 1)).astype(BF16)
    qw = lat + HEAD_DIM
    lane = lax.broadcasted_iota(jnp.int32, (qp_ref.shape[0], HEAD_DIM), 1)
    for p in range(n_heads // 2):
        pe = _rope64x2(qp_ref[:, p * HEAD_DIM:(p + 1) * HEAD_DIM], cos4, slo, shi)
        for half in range(2):
            h = 2 * p + half
            q_lat = jnp.dot(qn_ref[:, h * HEAD_DIM:(h + 1) * HEAD_DIM].astype(BF16), wuk_ref[h],
                            preferred_element_type=F32)
            q_ref[:, h * qw:h * qw + lat] = q_lat.astype(BF16)
            keep = (lane < HEAD_DIM // 2) if half == 0 else (lane >= HEAD_DIM // 2)
            q_ref[:, h * qw + lat:(h + 1) * qw] = jnp.where(keep, pe, 0.0).astype(BF16)


def mla_prep(proj, cos4, sin_lo, sin_hi, kv_norm, w_uk_t, n_heads, lat):
    M = proj.shape[0]
    tm = _tile(M, 272, 16)
    qw = lat + HEAD_DIM
    kern = functools.partial(_mla_prep_kernel, n_heads=n_heads, lat=lat)
    row = lambda w, j: pl.BlockSpec((tm, w), lambda i, j=j: (i, j))
    const1 = lambda shape: pl.BlockSpec(shape, lambda i: (0,) * len(shape))
    nope_w, pe_w = n_heads * HEAD_DIM, n_heads * HEAD_DIM // 2
    return pl.pallas_call(
        kern,
        out_shape=(jax.ShapeDtypeStruct((M, qw), F32),
                   jax.ShapeDtypeStruct((M, qw), BF16),
                   jax.ShapeDtypeStruct((M, n_heads * qw), BF16)),
        grid=(M // tm,),
        in_specs=[row(nope_w, 4), row(pe_w, 10), row(qw, 15),
                  row(HEAD_DIM, 0), row(HEAD_DIM, 0), row(HEAD_DIM, 0),
                  const1((1, lat)), const1(w_uk_t.shape)],
        out_specs=(row(qw, 0), row(qw, 0), row(n_heads * qw, 0)),
        compiler_params=_params(("parallel",), 32),
        name="mla_prep",
    )(proj, proj, proj, cos4, sin_lo, sin_hi, kv_norm, w_uk_t)


def _mla_prompt_kernel(q_ref, kb_ref, o_ref, qt_ref, m_ref, l_ref, acc_ref, *, n_heads, lat, tq, scale):
    i = pl.program_id(1)
    qw = lat + HEAD_DIM
    cols = n_heads * tq
    for h in range(n_heads):
        qt_ref[:, h * tq:(h + 1) * tq] = q_ref[:, h * qw:(h + 1) * qw].astype(F32).T.astype(BF16)
    m_ref[...] = jnp.full(m_ref.shape, NEG_INF, F32)
    l_ref[...] = jnp.zeros_like(l_ref)
    acc_ref[...] = jnp.zeros_like(acc_ref)

    def chunk(j, masked):
        kblk = kb_ref[pl.ds(pl.multiple_of(j * tq, tq), tq), :]
        s = jnp.dot(kblk, qt_ref[...], preferred_element_type=F32) * scale
        if masked:
            kloc = lax.broadcasted_iota(jnp.int32, (tq, cols), 0)
            qloc = lax.broadcasted_iota(jnp.int32, (tq, cols), 1) & (tq - 1)
            s = jnp.where(kloc <= qloc, s, NEG_INF)
        m_old = m_ref[...]
        m_new = jnp.maximum(m_old, jnp.max(s, axis=0, keepdims=True))
        alpha = jnp.exp(m_old - m_new)
        p = jnp.exp(s - m_new)
        l_ref[...] = alpha * l_ref[...] + jnp.sum(p, axis=0, keepdims=True)
        m_ref[...] = m_new
        vt = kblk[:, 0:lat].astype(F32).T.astype(BF16)
        acc_ref[...] = alpha * acc_ref[...] + jnp.dot(vt, p.astype(BF16), preferred_element_type=F32)

    def body(j, carry):
        chunk(j, False)
        return carry

    lax.fori_loop(0, i, body, 0)
    chunk(i, True)
    ot = acc_ref[...] / l_ref[...]
    for h in range(n_heads):
        o_ref[:, h * lat:(h + 1) * lat] = ot[:, h * tq:(h + 1) * tq].T.astype(BF16)


def mla_prompt(q, kb, B, L, n_heads, lat, scale):
    qw = lat + HEAD_DIM
    tq = _tile(L, 256, 128)
    assert tq & (tq - 1) == 0
    nq = L // tq
    cols = n_heads * tq
    kern = functools.partial(_mla_prompt_kernel, n_heads=n_heads, lat=lat, tq=tq, scale=scale)
    return pl.pallas_call(
        kern,
        out_shape=jax.ShapeDtypeStruct((B * L, n_heads * lat), BF16),
        grid=(B, nq),
        in_specs=[pl.BlockSpec((tq, n_heads * qw), lambda b, i: (b * nq + i, 0)),
                  pl.BlockSpec((L, qw), lambda b, i: (b, 0))],
        out_specs=pl.BlockSpec((tq, n_heads * lat), lambda b, i: (b * nq + i, 0)),
        scratch_shapes=[pltpu.VMEM((qw, cols), BF16),
                        pltpu.VMEM((1, cols), F32),
                        pltpu.VMEM((1, cols), F32),
                        pltpu.VMEM((lat, cols), F32)],
        compiler_params=_params(("parallel", "arbitrary"), 48),
        name="mla_prompt",
    )(q, kb)


def _mla_sample_kernel(pt_ref, q_ref, kn_ref, *refs, n_pages_step, lat, n_heads, seq_len, scale):
    P = n_pages_step
    kv_refs, kr_refs = refs[0:P], refs[P:2 * P]
    o_ref, cbuf_ref, rbuf_ref, m_ref, l_ref, acc_ref = refs[2 * P:]
    j = pl.program_id(1)
    page = kv_refs[0].shape[0]
    rope_w = kr_refs[0].shape[0]

    @pl.when(j == 0)
    def _():
        m_ref[...] = jnp.full(m_ref.shape, NEG_INF, F32)
        l_ref[...] = jnp.zeros_like(l_ref)
        acc_ref[...] = jnp.zeros_like(acc_ref)

    for p in range(P):
        cbuf_ref[p * page:(p + 1) * page, :] = kv_refs[p][...].astype(BF16)
        krt = kr_refs[p][...].astype(BF16)
        rbuf_ref[0:rope_w, p * page:(p + 1) * page] = krt
        rbuf_ref[rope_w:2 * rope_w, p * page:(p + 1) * page] = krt

    def update(s, vals):
        m_old = m_ref[...]
        m_new = jnp.maximum(m_old, jnp.max(s, axis=1, keepdims=True))
        alpha = jnp.exp(m_old - m_new)
        p = jnp.exp(s - m_new)
        l_ref[...] = alpha * l_ref[...] + jnp.sum(p, axis=1, keepdims=True)
        acc_ref[...] = alpha * acc_ref[...] + jnp.dot(p.astype(BF16), vals, preferred_element_type=F32)
        m_ref[...] = m_new

    q = q_ref[...]
    cb = cbuf_ref[...]
    s = (lax.dot_general(q[:, 0:lat], cb, NT_DIMS, preferred_element_type=F32)
         + jnp.dot(q[:, lat:], rbuf_ref[...], preferred_element_type=F32)) * scale
    update(s, cb)

    @pl.when(j == pl.num_programs(1) - 1)
    def _():
        kn = kn_ref[...]
        sn = lax.dot_general(q, kn, NT_DIMS, preferred_element_type=F32) * scale
        t_q = jnp.right_shift(lax.broadcasted_iota(jnp.int32, sn.shape, 0), n_heads.bit_length() - 1)
        t_k = lax.broadcasted_iota(jnp.int32, sn.shape, 1)
        sn = jnp.where(t_k <= t_q, sn, NEG_INF)
        update(sn, kn[:, 0:lat])
        o_ref[...] = (acc_ref[...] / l_ref[...]).astype(BF16)


def mla_sample(q_s, kb_new, cache_kv, cache_krt, page_table, layer, n_heads, lat, scale):
    DB, rows, qw = q_s.shape
    n_pages = page_table.shape[1]
    page = cache_kv.shape[2]
    rope_w = cache_krt.shape[2]
    P = _tile(n_pages, MAX_PAGES_PER_STEP, 1)
    seq_len = rows // n_heads
    assert seq_len <= page and n_heads & (n_heads - 1) == 0 and 2 * rope_w == HEAD_DIM
    kern = functools.partial(_mla_sample_kernel, n_pages_step=P, lat=lat, n_heads=n_heads,
                             seq_len=seq_len, scale=scale)
    kv_specs = [pl.BlockSpec((None, None, page, lat),
                             lambda b, j, pt, p=p: (layer, pt[b * n_pages + j * P + p], 0, 0)) for p in range(P)]
    kr_specs = [pl.BlockSpec((None, None, rope_w, page),
                             lambda b, j, pt, p=p: (layer, pt[b * n_pages + j * P + p], 0, 0)) for p in range(P)]
    grid_spec = pltpu.PrefetchScalarGridSpec(
        num_scalar_prefetch=1,
        grid=(DB, n_pages // P),
        in_specs=[pl.BlockSpec((None, rows, qw), lambda b, j, pt: (b, 0, 0)),
                  pl.BlockSpec((None, page, qw), lambda b, j, pt: (b, 0, 0))] + kv_specs + kr_specs,
        out_specs=pl.BlockSpec((None, rows, lat), lambda b, j, pt: (b, 0, 0)),
        scratch_shapes=[pltpu.VMEM((P * page, lat), BF16),
                        pltpu.VMEM((2 * rope_w, P * page), BF16),
                        pltpu.VMEM((rows, 1), F32),
                        pltpu.VMEM((rows, 1), F32),
                        pltpu.VMEM((rows, lat), F32)])
    return pl.pallas_call(
        kern,
        out_shape=jax.ShapeDtypeStruct((DB, rows, lat), BF16),
        grid_spec=grid_spec,
        compiler_params=_params(("parallel", "arbitrary"), 56),
        name="mla_sample",
    )(page_table.reshape(-1), q_s, kb_new, *([cache_kv] * P), *([cache_krt] * P))


def _rope_tables(pos, dim):
    inv = ROPE_BASE ** (-jnp.arange(0, dim, 2, dtype=F32) / dim)
    ang = pos.astype(F32)[:, None] * inv[None, :]
    return jnp.cos(ang), jnp.sin(ang)


def _relayout_w_in(w_in, n_mla, nope, rope_w, lat):
    depth, D, _ = w_in.shape
    main = 4 * (D // 4) + D
    n_gate = 2 * (D // (4 * HEAD_DIM))
    w = w_in.astype(BF16)
    gates = w[..., main:main + n_gate]
    q0 = main + n_gate
    mlq = w[..., q0:q0 + n_mla * (nope + rope_w)].reshape(depth, D, n_mla, nope + rope_w)
    q_nope = mlq[..., :nope].reshape(depth, D, n_mla * nope)
    q_pe = mlq[..., nope:].reshape(depth, D, n_mla * rope_w)
    k0 = q0 + n_mla * (nope + rope_w)
    mkv = w[..., k0:k0 + lat + rope_w]
    zeros = lambda n: jnp.zeros((depth, D, n), BF16)
    return jnp.concatenate([w[..., :main], q_nope, q_pe, gates, zeros(HEAD_DIM - n_gate),
                            mkv, zeros(HEAD_DIM - rope_w)], axis=-1)


def kernel(x_prompt, x_sample, cache_mla_kv, cache_mla_kr, page_table, state_ret, state_mlstm_C, state_mlstm_n, state_mlstm_m, state_mlstm_conv, norm_attn, norm_mlp, norm_final, w_in, ret_gn, mlstm_conv_w, mlstm_conv_b, mlstm_gate_b, mlstm_gn, mla_kv_norm, w_uk, w_uv, w_out, w_up, w_down):
    B, L, D = x_prompt.shape
    DB, T, _ = x_sample.shape
    depth = w_in.shape[0]
    n_heads = D // (4 * HEAD_DIM)
    W = n_heads * HEAD_DIM
    n_mla = D // (2 * HEAD_DIM)
    lat = mla_kv_norm.shape[1]
    rope_w = cache_mla_kr.shape[3]
    page = cache_mla_kv.shape[2]
    past_len = page_table.shape[1] * page
    assert D == 2048 and lat == 2 * HEAD_DIM and rope_w == HEAD_DIM // 2 and T >= CONV_W - 1 and L % min(CHUNK, L) == 0
    mla_scale = (HEAD_DIM + rope_w) ** -0.5
    G = 8
    n_p, n_s = B * L, DB * T
    gate_blk = (4 * W + 4 * W + n_mla * HEAD_DIM + n_mla * rope_w) // HEAD_DIM
    assert n_p % (G * T) == 0 and DB % G == 0

    pos = jnp.concatenate([jnp.tile(jnp.arange(L, dtype=jnp.int32), B),
                           jnp.tile(past_len + jnp.arange(T, dtype=jnp.int32), DB)])
    cos_h, sin_h = _rope_tables(pos, HEAD_DIM)
    cos2 = jnp.concatenate([cos_h, cos_h], axis=1)
    sin2 = jnp.concatenate([-sin_h, sin_h], axis=1)
    cos_r, sin_r = _rope_tables(pos, rope_w)
    zero_r = jnp.zeros_like(sin_r)
    cos4 = jnp.concatenate([cos_r] * 4, axis=1)
    sin_lo = jnp.concatenate([-sin_r, zero_r, -sin_r, zero_r], axis=1)
    sin_hi = jnp.concatenate([zero_r, sin_r, zero_r, sin_r], axis=1)

    w_in_b = _relayout_w_in(w_in, n_mla, HEAD_DIM, rope_w, lat)
    w_out_b, w_up_b, w_down_b = w_out.astype(BF16), w_up.astype(BF16), w_down.astype(BF16)
    w_uk_t = jnp.transpose(w_uk, (0, 2, 3, 1)).astype(BF16)
    w_uv_t = jnp.transpose(w_uv, (0, 2, 1, 3)).astype(BF16)
    gate_b = jnp.pad(mlstm_gate_b, ((0, 0), (0, HEAD_DIM - mlstm_gate_b.shape[1])))

    conv_rows = jnp.pad(state_mlstm_conv, ((0, 0), (0, 0), (T - (CONV_W - 1), 0), (0, 0))).reshape(depth, n_s, 2 * W)
    n_state = state_mlstm_n.reshape(depth, DB, W)
    n_rows = jnp.repeat(n_state, T, axis=1)
    m_rows = jnp.repeat(jnp.pad(state_mlstm_m, ((0, 0), (0, 0), (0, HEAD_DIM - n_heads))), T, axis=1)

    cache_krt = jnp.swapaxes(cache_mla_kr, 2, 3)

    h = jnp.concatenate([x_prompt.reshape(n_p, D), x_sample.reshape(n_s, D)], axis=0)
    outs = [[] for _ in range(14)]
    for l in range(depth):
        proj = norm_matmul(h, norm_attn[l][None], w_in_b, l)

        gn_r = ret_gn[l][None]
        yr_p, ret_p = retention_prompt(proj, cos2, sin2, gn_r, B, L, n_heads)
        yr_s, ret_s = retention_sample(proj, cos2, sin2, gn_r, state_ret, l, n_p, DB, T, n_heads, G)

        cw, cb, gb, gn_m = mlstm_conv_w[l], mlstm_conv_b[l][None], gate_b[l][None], mlstm_gn[l][None]
        ym_p, c_p, nn_p, mm_p = mlstm_prompt(proj, cw, cb, gb, gn_m, B, L, n_heads, gate_blk)
        ym_s, c_s, nn_s, mm_s = mlstm_sample(proj, cw, cb, gb, gn_m, conv_rows, state_mlstm_C, n_state,
                                             n_rows, m_rows, l, n_p, DB, T, n_heads, gate_blk, G)

        kvr, kb, q = mla_prep(proj, cos4, sin_lo, sin_hi, mla_kv_norm[l][None], w_uk_t[l], n_mla, lat)
        o_p = mla_prompt(q, kb, B, L, n_mla, lat, mla_scale)
        q_s = q[n_p:].reshape(DB, T * n_mla, lat + HEAD_DIM)
        kb_new = jnp.pad(kb[n_p:].reshape(DB, T, lat + HEAD_DIM), ((0, 0), (0, page - T), (0, 0)))
        o_s = mla_sample(q_s, kb_new, cache_mla_kv, cache_krt, page_table, l, n_mla, lat, mla_scale)

        y_ret = jnp.concatenate([yr_p, yr_s], axis=0)
        y_m = jnp.concatenate([ym_p, ym_s], axis=0)
        o_lat = jnp.concatenate([o_p, o_s.reshape(n_s, n_mla * lat)], axis=0)
        h = outproj_residual(h, y_ret, y_m, o_lat, w_uv_t, w_out_b, l)
        h = mlp_residual(h, norm_mlp[l][None], w_up_b, w_down_b, l)

        n_in = proj.shape[1]
        conv_p = proj[:n_p].reshape(B, L, n_in)[:, L - (CONV_W - 1):, 4 * W:6 * W]
        conv_s = proj[n_p:].reshape(DB, T, n_in)[:, T - (CONV_W - 1):, 4 * W:6 * W]
        per_layer = (
            kvr[:n_p, :lat].reshape(B, L, lat), kvr[:n_p, lat:lat + rope_w].reshape(B, L, rope_w),
            kvr[n_p:, :lat].reshape(DB, T, lat), kvr[n_p:, lat:lat + rope_w].reshape(DB, T, rope_w),
            ret_p, ret_s, c_p, c_s,
            nn_p.reshape(B, n_heads, HEAD_DIM), nn_s.reshape(DB, n_heads, HEAD_DIM),
            mm_p[:, 0, :n_heads], mm_s.reshape(DB, T, HEAD_DIM)[:, T - 1, :n_heads],
            conv_p, conv_s,
        )
        for acc, val in zip(outs, per_layer):
            acc.append(val)

    y = final_norm(h, norm_final[None])
    return (y[:n_p].reshape(B, L, D), y[n_p:].reshape(DB, T, D)) + tuple(jnp.stack(o) for o in outs)
```

```python
import functools
import math

import numpy as np
import jax
import jax.numpy as jnp
from jax import lax
from jax.experimental import pallas as pl
from jax.experimental.pallas import tpu as pltpu

F32 = jnp.float32
BF16 = jnp.bfloat16

HEAD_DIM = 128
CONV_W = 4
CHUNK = 128
ROPE_BASE = 10000.0
EPS = 1e-6
QK_SCALE = HEAD_DIM ** -0.5
NEG_INF = float("-inf")

MAX_PAGES_IN_VMEM = 64

NT_DIMS = (((1,), (1,)), ((), ()))
TN_DIMS = (((0,), (0,)), ((), ()))


def _params(semantics, vmem_mb):
    return pltpu.CompilerParams(dimension_semantics=semantics, vmem_limit_bytes=vmem_mb << 20)


def _tile(n, target, mult):
    best = None
    for t in range(mult, min(n, target) + 1, mult):
        if n % t == 0:
            best = t
    assert best is not None, (n, target, mult)
    return best


def _rms(x, g):
    ms = jnp.mean(x * x, axis=-1, keepdims=True)
    return (x * lax.rsqrt(ms + EPS)) * g


def _head_ln(y, g):
    mu = jnp.mean(y, axis=-1, keepdims=True)
    yc = y - mu
    var = jnp.mean(yc * yc, axis=-1, keepdims=True)
    return (yc * lax.rsqrt(var + EPS)) * g


def _norm_matmul_kernel(x_ref, g_ref, w_ref, o_ref, xn_ref):
    @pl.when(pl.program_id(1) == 0)
    def _():
        xn_ref[...] = _rms(x_ref[...], g_ref[...]).astype(BF16)

    o_ref[...] = jnp.dot(xn_ref[...], w_ref[...], preferred_element_type=F32)


def norm_matmul(x, g, w, layer):
    M, D = x.shape
    N = w.shape[2]
    tm = _tile(M, 1088, 16)
    tn = _tile(N, 512, 128)
    return pl.pallas_call(
        _norm_matmul_kernel,
        out_shape=jax.ShapeDtypeStruct((M, N), F32),
        grid=(M // tm, N // tn),
        in_specs=[
            pl.BlockSpec((tm, D), lambda i, j: (i, 0)),
            pl.BlockSpec((1, D), lambda i, j: (0, 0)),
            pl.BlockSpec((None, D, tn), lambda i, j: (layer, 0, j)),
        ],
        out_specs=pl.BlockSpec((tm, tn), lambda i, j: (i, j)),
        scratch_shapes=[pltpu.VMEM((tm, D), BF16)],
        compiler_params=_params(("parallel", "arbitrary"), 48),
        name="norm_inproj",
    )(x, g, w)


def _mlp_kernel(x_ref, g_ref, wu_ref, wd_ref, o_ref, xn_ref):
    @pl.when(pl.program_id(1) == 0)
    def _():
        x = x_ref[...]
        xn_ref[...] = _rms(x, g_ref[...]).astype(BF16)
        o_ref[...] = x

    u = jnp.dot(xn_ref[...], wu_ref[...], preferred_element_type=F32)
    a = jnp.maximum(u, 0.0)
    o_ref[...] += jnp.dot((a * a).astype(BF16), wd_ref[...], preferred_element_type=F32)


def mlp_residual(x, g, w_up, w_down, layer):
    M, D = x.shape
    F = w_up.shape[2]
    tm = _tile(M, 1088, 16)
    tf = _tile(F, 512, 128)
    return pl.pallas_call(
        _mlp_kernel,
        out_shape=jax.ShapeDtypeStruct((M, D), F32),
        grid=(M // tm, F // tf),
        in_specs=[
            pl.BlockSpec((tm, D), lambda i, f: (i, 0), pipeline_mode=pl.Buffered(1)),
            pl.BlockSpec((1, D), lambda i, f: (0, 0)),
            pl.BlockSpec((None, D, tf), lambda i, f: (layer, 0, f)),
            pl.BlockSpec((None, tf, D), lambda i, f: (layer, f, 0)),
        ],
        out_specs=pl.BlockSpec((tm, D), lambda i, f: (i, 0)),
        scratch_shapes=[pltpu.VMEM((tm, D), BF16)],
        compiler_params=_params(("parallel", "arbitrary"), 56),
        name="mlp",
    )(x, g, w_up, w_down)


def _outproj_kernel(x_ref, yrp_ref, yrs_ref, ymp_ref, yms_ref, op_ref, os_ref, wuv_ref, w_ref, out_ref, ymla_ref,
                    *, n_mla, w_ret, w_m, prompt_tiles):
    lat = wuv_ref.shape[1]
    is_prompt = pl.program_id(0) < prompt_tiles
    o_lat = jnp.where(is_prompt, op_ref[...], os_ref[...])
    for h in range(n_mla):
        yh = jnp.dot(o_lat[:, h * lat:(h + 1) * lat], wuv_ref[h], preferred_element_type=F32)
        ymla_ref[:, h * HEAD_DIM:(h + 1) * HEAD_DIM] = yh.astype(BF16)
    acc = x_ref[...]
    acc += jnp.dot(jnp.where(is_prompt, yrp_ref[...], yrs_ref[...]), w_ref[0:w_ret, :], preferred_element_type=F32)
    acc += jnp.dot(jnp.where(is_prompt, ymp_ref[...], yms_ref[...]), w_ref[w_ret:w_ret + w_m, :],
                   preferred_element_type=F32)
    acc += jnp.dot(ymla_ref[...], w_ref[w_ret + w_m:, :], preferred_element_type=F32)
    out_ref[...] = acc


def outproj_residual(x, y_ret, y_m, o_lat, w_uv, w_out, layer):
    M, D = x.shape
    _, n_mla, lat, _ = w_uv.shape
    n_p, n_s = o_lat[0].shape[0], o_lat[1].shape[0]
    w_ret, w_m = y_ret[0].shape[1], y_m[0].shape[1]
    tm = _tile(math.gcd(n_p, n_s), 272, 16)
    prompt_tiles = n_p // tm
    kern = functools.partial(_outproj_kernel, n_mla=n_mla, w_ret=w_ret, w_m=w_m, prompt_tiles=prompt_tiles)

    def pair(width):
        return [pl.BlockSpec((tm, width), lambda i: (jnp.minimum(i, prompt_tiles - 1), 0)),
                pl.BlockSpec((tm, width), lambda i: (jnp.maximum(i - prompt_tiles, 0), 0))]

    return pl.pallas_call(
        kern,
        out_shape=jax.ShapeDtypeStruct((M, D), F32),
        grid=(M // tm,),
        in_specs=[pl.BlockSpec((tm, D), lambda i: (i, 0))] + pair(w_ret) + pair(w_m) + pair(n_mla * lat) + [
            pl.BlockSpec((None,) + w_uv.shape[1:], lambda i: (layer, 0, 0, 0)),
            pl.BlockSpec((None,) + w_out.shape[1:], lambda i: (layer, 0, 0)),
        ],
        out_specs=pl.BlockSpec((tm, D), lambda i: (i, 0)),
        scratch_shapes=[pltpu.VMEM((tm, n_mla * HEAD_DIM), BF16)],
        compiler_params=_params(("parallel",), 48),
        name="outproj",
    )(x, *y_ret, *y_m, *o_lat, w_uv, w_out)


def _final_norm_kernel(x_ref, g_ref, op_ref, os_ref, *, prompt_tiles):
    y = _rms(x_ref[...], g_ref[...])
    i = pl.program_id(0)

    @pl.when(i < prompt_tiles)
    def _():
        op_ref[...] = y

    @pl.when(i >= prompt_tiles)
    def _():
        os_ref[...] = y


def final_norm(x, g, n_p):
    M, D = x.shape
    n_s = M - n_p
    tm = _tile(math.gcd(n_p, n_s), 544, 8)
    prompt_tiles = n_p // tm
    return pl.pallas_call(
        functools.partial(_final_norm_kernel, prompt_tiles=prompt_tiles),
        out_shape=(jax.ShapeDtypeStruct((n_p, D), F32), jax.ShapeDtypeStruct((n_s, D), F32)),
        grid=(M // tm,),
        in_specs=[pl.BlockSpec((tm, D), lambda i: (i, 0)), pl.BlockSpec((1, D), lambda i: (0, 0))],
        out_specs=(pl.BlockSpec((tm, D), lambda i: (jnp.minimum(i, prompt_tiles - 1), 0)),
                   pl.BlockSpec((tm, D), lambda i: (jnp.maximum(i - prompt_tiles, 0), 0))),
        compiler_params=_params(("arbitrary",), 32),
        name="final_norm",
    )(x, g)


def _rope128(x, cos2, sin2):
    return x * cos2 + pltpu.roll(x, HEAD_DIM // 2, 1) * sin2


def _ret_block(q_ref, k_ref, v_ref, g_ref, cos_ref, sin_ref, dm_ref, qd_ref, kd_ref, gn_ref, y_ref,
               load_state, store_state, rowseq, n_heads, n_sub, cdec):
    cos2, sin2 = cos_ref[...], sin_ref[...]
    for h in range(n_heads):
        sl = slice(h * HEAD_DIM, (h + 1) * HEAD_DIM)
        qr = _rope128(q_ref[:, sl], cos2, sin2)
        kr = _rope128(k_ref[:, sl], cos2, sin2) * QK_SCALE
        vb = v_ref[:, sl].astype(BF16)
        a = lax.dot_general(qr.astype(BF16), kr.astype(BF16), NT_DIMS, preferred_element_type=F32) * dm_ref[h]
        o = jnp.dot(a.astype(BF16), vb, preferred_element_type=F32)
        qs = (qr * qd_ref[:, sl]).astype(BF16)
        ks = kr * kd_ref[:, sl]
        for i in range(n_sub):
            s_old = load_state(i, h)
            inter = jnp.dot(qs, s_old.astype(BF16), preferred_element_type=F32)
            if n_sub == 1:
                o = o + inter
                ks_i = ks
            else:
                sel = rowseq == float(i)
                o = o + jnp.where(sel, inter, 0.0)
                ks_i = jnp.where(sel, ks, 0.0)
            kv = lax.dot_general(ks_i.astype(BF16), vb, TN_DIMS, preferred_element_type=F32)
            store_state(i, h, s_old * cdec[h] + kv)
        yn = _head_ln(o, gn_ref[:, sl])
        y_ref[:, sl] = (jax.nn.silu(g_ref[:, sl]) * yn).astype(BF16)


def _ret_prompt_kernel(q_ref, k_ref, v_ref, g_ref, cos_ref, sin_ref, dm_ref, qd_ref, kd_ref, gn_ref,
                       y_ref, so_ref, s_ref, *, n_heads, cdec):
    c = pl.program_id(1)

    @pl.when(c == 0)
    def _():
        s_ref[...] = jnp.zeros_like(s_ref)

    def load_state(i, h):
        return s_ref[h]

    def store_state(i, h, val):
        s_ref[h] = val

    _ret_block(q_ref, k_ref, v_ref, g_ref, cos_ref, sin_ref, dm_ref, qd_ref, kd_ref, gn_ref, y_ref,
               load_state, store_state, None, n_heads, 1, cdec)

    @pl.when(c == pl.num_programs(1) - 1)
    def _():
        so_ref[...] = s_ref[...]


def _ret_sample_kernel(q_ref, k_ref, v_ref, g_ref, cos_ref, sin_ref, dm_ref, qd_ref, kd_ref, gn_ref,
                       rs_ref, s0_ref, y_ref, so_ref, *, n_heads, n_sub, cdec):
    def load_state(i, h):
        return s0_ref[i, h]

    def store_state(i, h, val):
        so_ref[i, h] = val

    _ret_block(q_ref, k_ref, v_ref, g_ref, cos_ref, sin_ref, dm_ref, qd_ref, kd_ref, gn_ref, y_ref,
               load_state, store_state, rs_ref[...], n_heads, n_sub, cdec)


def _ret_decay_tables(n_heads, chunk, n_sub):
    lg = np.log1p(-np.exp2(-5.0 - np.arange(n_heads, dtype=np.float64)))
    idx = np.arange(chunk, dtype=np.float64)
    diff = idx[:, None] - idx[None, :]
    dm1 = np.where(diff[None] >= 0, np.exp(np.maximum(diff, 0.0)[None] * lg[:, None, None]), 0.0)
    rows = n_sub * chunk
    dm = np.zeros((n_heads, rows, rows))
    for i in range(n_sub):
        dm[:, i * chunk:(i + 1) * chunk, i * chunk:(i + 1) * chunk] = dm1
    q_dec = np.exp((idx[:, None] + 1.0) * lg[None, :])
    k_dec = np.exp((chunk - 1.0 - idx)[:, None] * lg[None, :])
    qd = np.tile(np.repeat(q_dec, HEAD_DIM, axis=1), (n_sub, 1))
    kd = np.tile(np.repeat(k_dec, HEAD_DIM, axis=1), (n_sub, 1))
    cdec = tuple(float(np.float32(v)) for v in np.exp(chunk * lg))
    return jnp.asarray(dm, F32), jnp.asarray(qd, F32), jnp.asarray(kd, F32), cdec


def retention_prompt(proj, cos2, sin2, gn, B, L, n_heads):
    W = n_heads * HEAD_DIM
    C = min(CHUNK, L)
    nc = L // C
    dm, qd, kd, cdec = _ret_decay_tables(n_heads, C, 1)
    kern = functools.partial(_ret_prompt_kernel, n_heads=n_heads, cdec=cdec)
    col = lambda j: pl.BlockSpec((C, W), lambda b, c, j=j: (b * nc + c, j))
    const2 = lambda shape: pl.BlockSpec(shape, lambda b, c: (0,) * len(shape))
    return pl.pallas_call(
        kern,
        out_shape=(jax.ShapeDtypeStruct((B * L, W), BF16),
                   jax.ShapeDtypeStruct((B, n_heads, HEAD_DIM, HEAD_DIM), F32)),
        grid=(B, nc),
        in_specs=[col(0), col(1), col(2), col(3),
                  pl.BlockSpec((C, HEAD_DIM), lambda b, c: (b * nc + c, 0)),
                  pl.BlockSpec((C, HEAD_DIM), lambda b, c: (b * nc + c, 0)),
                  const2(dm.shape), const2(qd.shape), const2(kd.shape), const2((1, W))],
        out_specs=(pl.BlockSpec((C, W), lambda b, c: (b * nc + c, 0)),
                   pl.BlockSpec((None, n_heads, HEAD_DIM, HEAD_DIM), lambda b, c: (b, 0, 0, 0))),
        scratch_shapes=[pltpu.VMEM((n_heads, HEAD_DIM, HEAD_DIM), F32)],
        compiler_params=_params(("parallel", "arbitrary"), 32),
        name="retention_prompt",
    )(proj, proj, proj, proj, cos2, sin2, dm, qd, kd, gn)


def retention_sample(proj, cos2, sin2, gn, state, layer, row0, DB, T, n_heads, G):
    W = n_heads * HEAD_DIM
    R = G * T
    rb0 = row0 // R
    dm, qd, kd, cdec = _ret_decay_tables(n_heads, T, G)
    rowseq = jnp.asarray(np.repeat(np.arange(G, dtype=np.float32), T)[:, None] * np.ones((1, HEAD_DIM), np.float32))
    kern = functools.partial(_ret_sample_kernel, n_heads=n_heads, n_sub=G, cdec=cdec)
    col = lambda j: pl.BlockSpec((R, W), lambda g, j=j: (rb0 + g, j))
    const1 = lambda shape: pl.BlockSpec(shape, lambda g: (0,) * len(shape))
    return pl.pallas_call(
        kern,
        out_shape=(jax.ShapeDtypeStruct((DB * T, W), BF16),
                   jax.ShapeDtypeStruct((DB, n_heads, HEAD_DIM, HEAD_DIM), F32)),
        grid=(DB // G,),
        in_specs=[col(0), col(1), col(2), col(3),
                  pl.BlockSpec((R, HEAD_DIM), lambda g: (rb0 + g, 0)),
                  pl.BlockSpec((R, HEAD_DIM), lambda g: (rb0 + g, 0)),
                  const1(dm.shape), const1(qd.shape), const1(kd.shape), const1((1, W)),
                  const1(rowseq.shape),
                  pl.BlockSpec((None, G, n_heads, HEAD_DIM, HEAD_DIM), lambda g: (layer, g, 0, 0, 0))],
        out_specs=(pl.BlockSpec((R, W), lambda g: (g, 0)),
                   pl.BlockSpec((G, n_heads, HEAD_DIM, HEAD_DIM), lambda g: (g, 0, 0, 0))),
        compiler_params=_params(("parallel",), 32),
        name="retention_sample",
    )(proj, proj, proj, proj, cos2, sin2, dm, qd, kd, gn, rowseq, state)


def _mlstm_block(u, v_ref, o_ref, gates, mprev_ref, mout_ref, nrows, tri, eye, lastm, gn_ref, y_ref,
                 load_c, store_c, store_n, rowseq, n_heads, n_sub, seq_len):
    W = n_heads * HEAD_DIM
    for h in range(n_heads):
        sl = slice(h * HEAD_DIM, (h + 1) * HEAD_DIM)
        q = u[:, h * HEAD_DIM:(h + 1) * HEAD_DIM]
        k = u[:, W + h * HEAD_DIM:W + (h + 1) * HEAD_DIM] * QK_SCALE
        qb, kb = q.astype(BF16), k.astype(BF16)
        vb = v_ref[:, sl].astype(BF16)
        ig_col = gates[:, h:h + 1]
        lf_col = gates[:, n_heads + h:n_heads + h + 1]
        m_col = mprev_ref[:, h:h + 1]
        f_row = jnp.sum(jnp.where(eye, lf_col, 0.0), axis=0, keepdims=True)
        bt_col = jnp.sum(jnp.where(tri, f_row, 0.0), axis=1, keepdims=True)
        bt_row = jnp.sum(jnp.where(eye, bt_col, 0.0), axis=0, keepdims=True)
        it_row = jnp.sum(jnp.where(eye, ig_col, 0.0), axis=0, keepdims=True)
        logw = jnp.where(tri, bt_col - bt_row + it_row, NEG_INF)
        log_inter = bt_col + m_col
        m_i = jnp.maximum(log_inter, jnp.max(logw, axis=1, keepdims=True))
        w = jnp.exp(logw - m_i)
        a = jnp.exp(log_inter - m_i)
        s = lax.dot_general(qb, kb, NT_DIMS, preferred_element_type=F32) * w
        num = jnp.dot(s.astype(BF16), vb, preferred_element_type=F32)
        den = jnp.sum(s, axis=1, keepdims=True) + a * jnp.sum(q * nrows(h), axis=1, keepdims=True)
        m_row = jnp.sum(jnp.where(eye, m_i, 0.0), axis=0, keepdims=True)
        m_last = jnp.sum(jnp.where(lastm, m_row, 0.0), axis=1, keepdims=True)
        bt_last = jnp.sum(jnp.where(lastm, bt_row, 0.0), axis=1, keepdims=True)
        wl = jnp.exp(bt_last - bt_col + ig_col - m_last)
        al = jnp.exp(bt_last + m_col - m_last)
        kw = k * wl
        inter_sum = None
        for i in range(n_sub):
            c_old = load_c(i, h)
            inter = jnp.dot(qb, c_old.astype(BF16), preferred_element_type=F32)
            if n_sub == 1:
                kw_i = kw
            else:
                sel = rowseq == float(i)
                inter = jnp.where(sel, inter, 0.0)
                kw_i = jnp.where(sel, kw, 0.0)
            inter_sum = inter if inter_sum is None else inter_sum + inter
            al_i = al[i * seq_len:i * seq_len + 1, :]
            kv = lax.dot_general(kw_i.astype(BF16), vb, TN_DIMS, preferred_element_type=F32)
            store_c(i, h, al_i * c_old + kv)
            store_n(i, h, al_i, jnp.sum(kw_i, axis=0, keepdims=True))
        num = num + a * inter_sum
        hh = num / jnp.maximum(jnp.abs(den), jnp.exp(-m_i))
        yn = _head_ln(hh, gn_ref[:, sl])
        y_ref[:, sl] = (jax.nn.sigmoid(o_ref[:, sl]) * yn).astype(BF16)
        mout_ref[:, h:h + 1] = m_last


def _gate_values(g_ref, gb_ref, n_heads):
    pre = g_ref[...] + gb_ref[...]
    lane = lax.broadcasted_iota(jnp.int32, pre.shape, 1)
    lf = jnp.minimum(pre, 0.0) - jnp.log1p(jnp.exp(-jnp.abs(pre)))
    return jnp.where(lane < n_heads, pre, lf)


def _conv_silu(taps, cw_ref, cb_ref):
    acc = cb_ref[...]
    for j in range(CONV_W):
        acc = acc + taps[j] * cw_ref[j:j + 1, :]
    return jax.nn.silu(acc)


def _mlstm_prompt_kernel(x_ref, v_ref, o_ref, g_ref, cw_ref, cb_ref, gb_ref, gn_ref, tri_ref, eye_ref,
                         last_ref, y_ref, co_ref, no_ref, mo_ref,
                         xbuf_ref, c_ref, n_ref, m_ref, *, n_heads):
    c = pl.program_id(1)
    C = x_ref.shape[0]

    @pl.when(c == 0)
    def _():
        xbuf_ref[0:8, :] = jnp.zeros((8, xbuf_ref.shape[1]), F32)
        c_ref[...] = jnp.zeros_like(c_ref)
        n_ref[...] = jnp.zeros_like(n_ref)
        m_ref[...] = jnp.zeros_like(m_ref)

    xbuf_ref[8:8 + C, :] = x_ref[...]
    taps = [xbuf_ref[8 - (CONV_W - 1) + j:8 - (CONV_W - 1) + j + C, :] for j in range(CONV_W)]
    u = _conv_silu(taps, cw_ref, cb_ref)
    xbuf_ref[0:8, :] = x_ref[C - 8:C, :]
    gates = _gate_values(g_ref, gb_ref, n_heads)
    tri, eye, lastm = tri_ref[...] > 0.5, eye_ref[...] > 0.5, last_ref[...] > 0.5

    def load_c(i, h):
        return c_ref[h]

    def store_c(i, h, val):
        c_ref[h] = val

    def store_n(i, h, al_i, ksum):
        n_ref[h] = al_i * n_ref[h] + ksum

    def nrows(h):
        return n_ref[h]

    _mlstm_block(u, v_ref, o_ref, gates, m_ref, m_ref, nrows, tri, eye, lastm, gn_ref, y_ref,
                 load_c, store_c, store_n, None, n_heads, 1, C)

    @pl.when(c == pl.num_programs(1) - 1)
    def _():
        co_ref[...] = c_ref[...]
        no_ref[...] = n_ref[...]
        mo_ref[...] = m_ref[0:8, :]


def _mlstm_sample_kernel(x_ref, v_ref, o_ref, g_ref, cw_ref, cb_ref, gb_ref, gn_ref, tri_ref, eye_ref,
                         last_ref, rs_ref, rt_ref, st_ref, c0_ref, n0_ref, nr_ref, m0_ref,
                         y_ref, co_ref, no_ref, mo_ref, *, n_heads, n_sub, seq_len):
    R = x_ref.shape[0]
    x = x_ref[...]
    st = st_ref[...]
    tpos = rt_ref[:, 0:1]
    taps = []
    for j in range(CONV_W):
        s = CONV_W - 1 - j
        if s == 0:
            taps.append(x)
        else:
            cur = pltpu.roll(x, s, 0)
            old = pltpu.roll(st, (s - seq_len) % R, 0)
            taps.append(jnp.where(tpos >= float(s), cur, old))
    u = _conv_silu(taps, cw_ref, cb_ref)
    gates = _gate_values(g_ref, gb_ref, n_heads)
    tri, eye, lastm = tri_ref[...] > 0.5, eye_ref[...] > 0.5, last_ref[...] > 0.5
    mo_ref[...] = jnp.zeros_like(mo_ref)

    def load_c(i, h):
        return c0_ref[i, h]

    def store_c(i, h, val):
        co_ref[i, h] = val

    def store_n(i, h, al_i, ksum):
        sl = slice(h * HEAD_DIM, (h + 1) * HEAD_DIM)
        no_ref[i:i + 1, sl] = al_i * n0_ref[i:i + 1, sl] + ksum

    def nrows(h):
        return nr_ref[:, h * HEAD_DIM:(h + 1) * HEAD_DIM]

    _mlstm_block(u, v_ref, o_ref, gates, m0_ref, mo_ref, nrows, tri, eye, lastm, gn_ref, y_ref,
                 load_c, store_c, store_n, rs_ref[...], n_heads, n_sub, seq_len)


def _seq_masks(n_sub, seq_len):
    rows = n_sub * seq_len
    r = np.arange(rows)
    seq, t = r // seq_len, r % seq_len
    same = seq[:, None] == seq[None, :]
    tri = (same & (t[None, :] <= t[:, None])).astype(np.float32)
    eye = np.eye(rows, dtype=np.float32)
    last = (r[None, :] == (seq[:, None] * seq_len + seq_len - 1)).astype(np.float32)
    return jnp.asarray(tri), jnp.asarray(eye), jnp.asarray(last)


def mlstm_prompt(proj, conv_w, conv_b, gate_b, gn, B, L, n_heads, gate_blk):
    W = n_heads * HEAD_DIM
    C = min(CHUNK, L)
    nc = L // C
    tri, eye, last = _seq_masks(1, C)
    kern = functools.partial(_mlstm_prompt_kernel, n_heads=n_heads)
    const2 = lambda shape: pl.BlockSpec(shape, lambda b, c: (0,) * len(shape))
    return pl.pallas_call(
        kern,
        out_shape=(jax.ShapeDtypeStruct((B * L, W), BF16),
                   jax.ShapeDtypeStruct((B, n_heads, HEAD_DIM, HEAD_DIM), F32),
                   jax.ShapeDtypeStruct((B, n_heads, 1, HEAD_DIM), F32),
                   jax.ShapeDtypeStruct((B, 8, HEAD_DIM), F32)),
        grid=(B, nc),
        in_specs=[pl.BlockSpec((C, 2 * W), lambda b, c: (b * nc + c, 2)),
                  pl.BlockSpec((C, W), lambda b, c: (b * nc + c, 6)),
                  pl.BlockSpec((C, W), lambda b, c: (b * nc + c, 7)),
                  pl.BlockSpec((C, HEAD_DIM), lambda b, c: (b * nc + c, gate_blk)),
                  const2((CONV_W, 2 * W)), const2((1, 2 * W)), const2((1, HEAD_DIM)), const2((1, W)),
                  const2(tri.shape), const2(eye.shape), const2(last.shape)],
        out_specs=(pl.BlockSpec((C, W), lambda b, c: (b * nc + c, 0)),
                   pl.BlockSpec((None, n_heads, HEAD_DIM, HEAD_DIM), lambda b, c: (b, 0, 0, 0)),
                   pl.BlockSpec((None, n_heads, 1, HEAD_DIM), lambda b, c: (b, 0, 0, 0)),
                   pl.BlockSpec((None, 8, HEAD_DIM), lambda b, c: (b, 0, 0))),
        scratch_shapes=[pltpu.VMEM((8 + C, 2 * W), F32),
                        pltpu.VMEM((n_heads, HEAD_DIM, HEAD_DIM), F32),
                        pltpu.VMEM((n_heads, 1, HEAD_DIM), F32),
                        pltpu.VMEM((C, HEAD_DIM), F32)],
        compiler_params=_params(("parallel", "arbitrary"), 32),
        name="mlstm_prompt",
    )(proj, proj, proj, proj, conv_w, conv_b, gate_b, gn, tri, eye, last)


def mlstm_sample(proj, conv_w, conv_b, gate_b, gn, conv_rows, c_state, n_state, n_rows, m_rows,
                 layer, row0, DB, T, n_heads, gate_blk, G):
    W = n_heads * HEAD_DIM
    R = G * T
    rb0 = row0 // R
    tri, eye, last = _seq_masks(G, T)
    ones = np.ones((1, HEAD_DIM), np.float32)
    rowseq = jnp.asarray(np.repeat(np.arange(G, dtype=np.float32), T)[:, None] * ones)
    rowt = jnp.asarray(np.tile(np.arange(T, dtype=np.float32), G)[:, None] * ones)
    kern = functools.partial(_mlstm_sample_kernel, n_heads=n_heads, n_sub=G, seq_len=T)
    const1 = lambda shape: pl.BlockSpec(shape, lambda g: (0,) * len(shape))
    return pl.pallas_call(
        kern,
        out_shape=(jax.ShapeDtypeStruct((DB * T, W), BF16),
                   jax.ShapeDtypeStruct((DB, n_heads, HEAD_DIM, HEAD_DIM), F32),
                   jax.ShapeDtypeStruct((DB, W), F32),
                   jax.ShapeDtypeStruct((DB * T, HEAD_DIM), F32)),
        grid=(DB // G,),
        in_specs=[pl.BlockSpec((R, 2 * W), lambda g: (rb0 + g, 2)),
                  pl.BlockSpec((R, W), lambda g: (rb0 + g, 6)),
                  pl.BlockSpec((R, W), lambda g: (rb0 + g, 7)),
                  pl.BlockSpec((R, HEAD_DIM), lambda g: (rb0 + g, gate_blk)),
                  const1((CONV_W, 2 * W)), const1((1, 2 * W)), const1((1, HEAD_DIM)), const1((1, W)),
                  const1(tri.shape), const1(eye.shape), const1(last.shape),
                  const1(rowseq.shape), const1(rowt.shape),
                  pl.BlockSpec((None, R, 2 * W), lambda g: (layer, g, 0)),
                  pl.BlockSpec((None, G, n_heads, HEAD_DIM, HEAD_DIM), lambda g: (layer, g, 0, 0, 0)),
                  pl.BlockSpec((None, G, W), lambda g: (layer, g, 0)),
                  pl.BlockSpec((None, R, W), lambda g: (layer, g, 0)),
                  pl.BlockSpec((None, R, HEAD_DIM), lambda g: (layer, g, 0))],
        out_specs=(pl.BlockSpec((R, W), lambda g: (g, 0)),
                   pl.BlockSpec((G, n_heads, HEAD_DIM, HEAD_DIM), lambda g: (g, 0, 0, 0)),
                   pl.BlockSpec((G, W), lambda g: (g, 0)),
                   pl.BlockSpec((R, HEAD_DIM), lambda g: (g, 0))),
        compiler_params=_params(("parallel",), 32),
        name="mlstm_sample",
    )(proj, proj, proj, proj, conv_w, conv_b, gate_b, gn, tri, eye, last, rowseq, rowt,
      conv_rows, c_state, n_state, n_rows, m_rows)


def _rope64x2(x, cos4, sin_lo, sin_hi):
    return x * cos4 + pltpu.roll(x, 96, 1) * sin_lo + pltpu.roll(x, 32, 1) * sin_hi


def _mla_prep_kernel(qn_ref, qp_ref, kv_ref, cos_ref, slo_ref, shi_ref, kvn_ref, wuk_ref,
                     kvr_ref, kb_ref, q_ref, *, n_heads, lat):
    cos4, slo, shi = cos_ref[...], slo_ref[...], shi_ref[...]
    c_new = _rms(kv_ref[:, 0:lat], kvn_ref[...])
    kr = _rope64x2(kv_ref[:, lat:lat + HEAD_DIM], cos4, slo, shi)
    kvr_ref[:, 0:lat] = c_new
    kvr_ref[:, lat:lat + HEAD_DIM] = kr
    kb_ref[:, 0:lat] = c_new.astype(BF16)
    kb_ref[:, lat:lat + HEAD_DIM] = (kr + pltpu.roll(kr, HEAD_DIM // 2, 1)).astype(BF16)
    qw = lat + HEAD_DIM
    lane = lax.broadcasted_iota(jnp.int32, (qp_ref.shape[0], HEAD_DIM), 1)
    for p in range(n_heads // 2):
        pe = _rope64x2(qp_ref[:, p * HEAD_DIM:(p + 1) * HEAD_DIM], cos4, slo, shi)
        for half in range(2):
            h = 2 * p + half
            q_lat = jnp.dot(qn_ref[:, h * HEAD_DIM:(h + 1) * HEAD_DIM].astype(BF16), wuk_ref[h],
                            preferred_element_type=F32)
            q_ref[:, h * qw:h * qw + lat] = q_lat.astype(BF16)
            keep = (lane < HEAD_DIM // 2) if half == 0 else (lane >= HEAD_DIM // 2)
            q_ref[:, h * qw + lat:(h + 1) * qw] = jnp.where(keep, pe, 0.0).astype(BF16)


def mla_prep(proj, cos4, sin_lo, sin_hi, kv_norm, w_uk_t, n_heads, lat):
    M = proj.shape[0]
    tm = _tile(M, 272, 16)
    qw = lat + HEAD_DIM
    kern = functools.partial(_mla_prep_kernel, n_heads=n_heads, lat=lat)
    row = lambda w, j: pl.BlockSpec((tm, w), lambda i, j=j: (i, j))
    const1 = lambda shape: pl.BlockSpec(shape, lambda i: (0,) * len(shape))
    nope_w, pe_w = n_heads * HEAD_DIM, n_heads * HEAD_DIM // 2
    return pl.pallas_call(
        kern,
        out_shape=(jax.ShapeDtypeStruct((M, qw), F32),
                   jax.ShapeDtypeStruct((M, qw), BF16),
                   jax.ShapeDtypeStruct((M, n_heads * qw), BF16)),
        grid=(M // tm,),
        in_specs=[row(nope_w, 4), row(pe_w, 10), row(qw, 15),
                  row(HEAD_DIM, 0), row(HEAD_DIM, 0), row(HEAD_DIM, 0),
                  const1((1, lat)), const1(w_uk_t.shape)],
        out_specs=(row(qw, 0), row(qw, 0), row(n_heads * qw, 0)),
        compiler_params=_params(("parallel",), 32),
        name="mla_prep",
    )(proj, proj, proj, cos4, sin_lo, sin_hi, kv_norm, w_uk_t)


def _mla_prompt_kernel(q_ref, kb_ref, o_ref, qt_ref, m_ref, l_ref, acc_ref, *, n_heads, lat, tq, scale):
    i = pl.program_id(1)
    qw = lat + HEAD_DIM
    cols = n_heads * tq
    for h in range(n_heads):
        qt_ref[:, h * tq:(h + 1) * tq] = q_ref[:, h * qw:(h + 1) * qw].astype(F32).T.astype(BF16)
    m_ref[...] = jnp.full(m_ref.shape, NEG_INF, F32)
    l_ref[...] = jnp.zeros_like(l_ref)
    acc_ref[...] = jnp.zeros_like(acc_ref)

    def chunk(j, masked):
        kblk = kb_ref[pl.ds(pl.multiple_of(j * tq, tq), tq), :]
        s = jnp.dot(kblk, qt_ref[...], preferred_element_type=F32) * scale
        if masked:
            kloc = lax.broadcasted_iota(jnp.int32, (tq, cols), 0)
            qloc = lax.broadcasted_iota(jnp.int32, (tq, cols), 1) & (tq - 1)
            s = jnp.where(kloc <= qloc, s, NEG_INF)
        m_old = m_ref[...]
        m_new = jnp.maximum(m_old, jnp.max(s, axis=0, keepdims=True))
        alpha = jnp.exp(m_old - m_new)
        p = jnp.exp(s - m_new)
        l_ref[...] = alpha * l_ref[...] + jnp.sum(p, axis=0, keepdims=True)
        m_ref[...] = m_new
        vt = kblk[:, 0:lat].astype(F32).T.astype(BF16)
        acc_ref[...] = alpha * acc_ref[...] + jnp.dot(vt, p.astype(BF16), preferred_element_type=F32)

    def body(j, carry):
        chunk(j, False)
        return carry

    lax.fori_loop(0, i, body, 0)
    chunk(i, True)
    ot = acc_ref[...] / l_ref[...]
    for h in range(n_heads):
        o_ref[:, h * lat:(h + 1) * lat] = ot[:, h * tq:(h + 1) * tq].T.astype(BF16)


def mla_prompt(q, kb, B, L, n_heads, lat, scale):
    qw = lat + HEAD_DIM
    tq = _tile(L, 256, 128)
    assert tq & (tq - 1) == 0
    nq = L // tq
    cols = n_heads * tq
    kern = functools.partial(_mla_prompt_kernel, n_heads=n_heads, lat=lat, tq=tq, scale=scale)
    return pl.pallas_call(
        kern,
        out_shape=jax.ShapeDtypeStruct((B * L, n_heads * lat), BF16),
        grid=(B, nq),
        in_specs=[pl.BlockSpec((tq, n_heads * qw), lambda b, i: (b * nq + i, 0)),
                  pl.BlockSpec((L, qw), lambda b, i: (b, 0))],
        out_specs=pl.BlockSpec((tq, n_heads * lat), lambda b, i: (b * nq + i, 0)),
        scratch_shapes=[pltpu.VMEM((qw, cols), BF16),
                        pltpu.VMEM((1, cols), F32),
                        pltpu.VMEM((1, cols), F32),
                        pltpu.VMEM((lat, cols), F32)],
        compiler_params=_params(("parallel", "arbitrary"), 48),
        name="mla_prompt",
    )(q, kb)


def _mla_sample_kernel(pt_ref, q_ref, kn_ref, kv_hbm, krt_hbm, o_ref, kvbuf, krbuf, cbuf_ref, rbuf_ref, sem,
                       *, layer, n_pages, lat, n_heads, scale):
    b = pl.program_id(0)
    nb = pl.num_programs(0)
    slot = b & 1
    page = kvbuf.shape[2]
    rope_w = krbuf.shape[2]

    def page_copies(seq, dst_slot):
        copies = []
        for p in range(n_pages):
            pg = pt_ref[seq * n_pages + p]
            copies.append(pltpu.make_async_copy(kv_hbm.at[layer, pg], kvbuf.at[dst_slot, p], sem.at[0, dst_slot]))
            copies.append(pltpu.make_async_copy(krt_hbm.at[layer, pg], krbuf.at[dst_slot, p], sem.at[1, dst_slot]))
        return copies

    @pl.when(b == 0)
    def _():
        for cp in page_copies(0, 0):
            cp.start()

    for cp in page_copies(b, slot):
        cp.wait()
    nxt = jnp.minimum(b + 1, nb - 1)
    for cp in page_copies(nxt, 1 - slot):
        cp.start()

    for p in range(n_pages):
        cbuf_ref[p * page:(p + 1) * page, :] = kvbuf[slot, p].astype(BF16)
        krt = krbuf[slot, p].astype(BF16)
        rbuf_ref[0:rope_w, p * page:(p + 1) * page] = krt
        rbuf_ref[rope_w:2 * rope_w, p * page:(p + 1) * page] = krt
    q = q_ref[...]
    cb = cbuf_ref[...]
    s = (lax.dot_general(q[:, 0:lat], cb, NT_DIMS, preferred_element_type=F32)
         + jnp.dot(q[:, lat:], rbuf_ref[...], preferred_element_type=F32)) * scale
    kn = kn_ref[...]
    sn = lax.dot_general(q, kn, NT_DIMS, preferred_element_type=F32) * scale
    t_q = jnp.right_shift(lax.broadcasted_iota(jnp.int32, sn.shape, 0), n_heads.bit_length() - 1)
    t_k = lax.broadcasted_iota(jnp.int32, sn.shape, 1)
    sn = jnp.where(t_k <= t_q, sn, NEG_INF)
    m = jnp.maximum(jnp.max(s, axis=1, keepdims=True), jnp.max(sn, axis=1, keepdims=True))
    pc = jnp.exp(s - m)
    pn = jnp.exp(sn - m)
    denom = jnp.sum(pc, axis=1, keepdims=True) + jnp.sum(pn, axis=1, keepdims=True)
    acc = (jnp.dot(pc.astype(BF16), cb, preferred_element_type=F32)
           + jnp.dot(pn.astype(BF16), kn[:, 0:lat], preferred_element_type=F32))
    o_ref[...] = (acc / denom).astype(BF16)

    @pl.when(b == nb - 1)
    def _():
        for cp in page_copies(nxt, 1 - slot):
            cp.wait()


def mla_sample(q_s, kb_new, cache_kv, cache_krt, page_table, layer, n_heads, lat, scale):
    DB, rows, qw = q_s.shape
    n_pages = page_table.shape[1]
    page = cache_kv.shape[2]
    rope_w = cache_krt.shape[2]
    seq_len = rows // n_heads
    assert seq_len <= page and n_heads & (n_heads - 1) == 0 and 2 * rope_w == HEAD_DIM
    assert n_pages <= MAX_PAGES_IN_VMEM, "a sequence's cache pages are held in VMEM twice (two slots)"
    kern = functools.partial(_mla_sample_kernel, layer=layer, n_pages=n_pages, lat=lat, n_heads=n_heads,
                             scale=scale)
    grid_spec = pltpu.PrefetchScalarGridSpec(
        num_scalar_prefetch=1,
        grid=(DB,),
        in_specs=[pl.BlockSpec((None, rows, qw), lambda b, pt: (b, 0, 0)),
                  pl.BlockSpec((None, page, qw), lambda b, pt: (b, 0, 0)),
                  pl.BlockSpec(memory_space=pl.ANY),
                  pl.BlockSpec(memory_space=pl.ANY)],
        out_specs=pl.BlockSpec((None, rows, lat), lambda b, pt: (b, 0, 0)),
        scratch_shapes=[pltpu.VMEM((2, n_pages, page, lat), F32),
                        pltpu.VMEM((2, n_pages, rope_w, page), F32),
                        pltpu.VMEM((n_pages * page, lat), BF16),
                        pltpu.VMEM((2 * rope_w, n_pages * page), BF16),
                        pltpu.SemaphoreType.DMA((2, 2))])
    return pl.pallas_call(
        kern,
        out_shape=jax.ShapeDtypeStruct((DB, rows, lat), BF16),
        grid_spec=grid_spec,
        compiler_params=_params(("arbitrary",), 56),
        name="mla_sample",
    )(page_table.reshape(-1), q_s, kb_new, cache_kv, cache_krt)


def _rope_tables(pos, dim):
    inv = ROPE_BASE ** (-jnp.arange(0, dim, 2, dtype=F32) / dim)
    ang = pos.astype(F32)[:, None] * inv[None, :]
    return jnp.cos(ang), jnp.sin(ang)


def _relayout_w_in(w_in, n_mla, nope, rope_w, lat):
    depth, D, _ = w_in.shape
    main = 4 * (D // 4) + D
    n_gate = 2 * (D // (4 * HEAD_DIM))
    w = w_in.astype(BF16)
    gates = w[..., main:main + n_gate]
    q0 = main + n_gate
    mlq = w[..., q0:q0 + n_mla * (nope + rope_w)].reshape(depth, D, n_mla, nope + rope_w)
    q_nope = mlq[..., :nope].reshape(depth, D, n_mla * nope)
    q_pe = mlq[..., nope:].reshape(depth, D, n_mla * rope_w)
    k0 = q0 + n_mla * (nope + rope_w)
    mkv = w[..., k0:k0 + lat + rope_w]
    zeros = lambda n: jnp.zeros((depth, D, n), BF16)
    return jnp.concatenate([w[..., :main], q_nope, q_pe, gates, zeros(HEAD_DIM - n_gate),
                            mkv, zeros(HEAD_DIM - rope_w)], axis=-1)


def kernel(x_prompt, x_sample, cache_mla_kv, cache_mla_kr, page_table, state_ret, state_mlstm_C, state_mlstm_n, state_mlstm_m, state_mlstm_conv, norm_attn, norm_mlp, norm_final, w_in, ret_gn, mlstm_conv_w, mlstm_conv_b, mlstm_gate_b, mlstm_gn, mla_kv_norm, w_uk, w_uv, w_out, w_up, w_down):
    B, L, D = x_prompt.shape
    DB, T, _ = x_sample.shape
    depth = w_in.shape[0]
    n_heads = D // (4 * HEAD_DIM)
    W = n_heads * HEAD_DIM
    n_mla = D // (2 * HEAD_DIM)
    lat = mla_kv_norm.shape[1]
    rope_w = cache_mla_kr.shape[3]
    page = cache_mla_kv.shape[2]
    past_len = page_table.shape[1] * page
    assert D == 2048 and lat == 2 * HEAD_DIM and rope_w == HEAD_DIM // 2 and T >= CONV_W - 1 and L % min(CHUNK, L) == 0
    mla_scale = (HEAD_DIM + rope_w) ** -0.5
    G = 8
    n_p, n_s = B * L, DB * T
    gate_blk = (4 * W + 4 * W + n_mla * HEAD_DIM + n_mla * rope_w) // HEAD_DIM
    assert n_p % (G * T) == 0 and DB % G == 0

    pos = jnp.concatenate([jnp.tile(jnp.arange(L, dtype=jnp.int32), B),
                           jnp.tile(past_len + jnp.arange(T, dtype=jnp.int32), DB)])
    cos_h, sin_h = _rope_tables(pos, HEAD_DIM)
    cos2 = jnp.concatenate([cos_h, cos_h], axis=1)
    sin2 = jnp.concatenate([-sin_h, sin_h], axis=1)
    cos_r, sin_r = _rope_tables(pos, rope_w)
    zero_r = jnp.zeros_like(sin_r)
    cos4 = jnp.concatenate([cos_r] * 4, axis=1)
    sin_lo = jnp.concatenate([-sin_r, zero_r, -sin_r, zero_r], axis=1)
    sin_hi = jnp.concatenate([zero_r, sin_r, zero_r, sin_r], axis=1)

    w_in_b = _relayout_w_in(w_in, n_mla, HEAD_DIM, rope_w, lat)
    w_out_b, w_up_b, w_down_b = w_out.astype(BF16), w_up.astype(BF16), w_down.astype(BF16)
    w_uk_t = jnp.transpose(w_uk, (0, 2, 3, 1)).astype(BF16)
    w_uv_t = jnp.transpose(w_uv, (0, 2, 1, 3)).astype(BF16)
    gate_b = jnp.pad(mlstm_gate_b, ((0, 0), (0, HEAD_DIM - mlstm_gate_b.shape[1])))

    conv_rows = jnp.pad(state_mlstm_conv, ((0, 0), (0, 0), (T - (CONV_W - 1), 0), (0, 0))).reshape(depth, n_s, 2 * W)
    n_state = state_mlstm_n.reshape(depth, DB, W)
    n_rows = jnp.repeat(n_state, T, axis=1)
    m_rows = jnp.repeat(jnp.pad(state_mlstm_m, ((0, 0), (0, 0), (0, HEAD_DIM - n_heads))), T, axis=1)

    cache_krt = jnp.swapaxes(cache_mla_kr, 2, 3)

    h = jnp.concatenate([x_prompt.reshape(n_p, D), x_sample.reshape(n_s, D)], axis=0)
    outs = [[] for _ in range(14)]
    for l in range(depth):
        proj = norm_matmul(h, norm_attn[l][None], w_in_b, l)

        gn_r = ret_gn[l][None]
        yr_p, ret_p = retention_prompt(proj, cos2, sin2, gn_r, B, L, n_heads)
        yr_s, ret_s = retention_sample(proj, cos2, sin2, gn_r, state_ret, l, n_p, DB, T, n_heads, G)

        cw, cb, gb, gn_m = mlstm_conv_w[l], mlstm_conv_b[l][None], gate_b[l][None], mlstm_gn[l][None]
        ym_p, c_p, nn_p, mm_p = mlstm_prompt(proj, cw, cb, gb, gn_m, B, L, n_heads, gate_blk)
        ym_s, c_s, nn_s, mm_s = mlstm_sample(proj, cw, cb, gb, gn_m, conv_rows, state_mlstm_C, n_state,
                                             n_rows, m_rows, l, n_p, DB, T, n_heads, gate_blk, G)

        kvr, kb, q = mla_prep(proj, cos4, sin_lo, sin_hi, mla_kv_norm[l][None], w_uk_t[l], n_mla, lat)
        o_p = mla_prompt(q, kb, B, L, n_mla, lat, mla_scale)
        q_s = q[n_p:].reshape(DB, T * n_mla, lat + HEAD_DIM)
        kb_new = jnp.pad(kb[n_p:].reshape(DB, T, lat + HEAD_DIM), ((0, 0), (0, page - T), (0, 0)))
        o_s = mla_sample(q_s, kb_new, cache_mla_kv, cache_krt, page_table, l, n_mla, lat, mla_scale)

        h = outproj_residual(h, (yr_p, yr_s), (ym_p, ym_s), (o_p, o_s.reshape(n_s, n_mla * lat)),
                             w_uv_t, w_out_b, l)
        h = mlp_residual(h, norm_mlp[l][None], w_up_b, w_down_b, l)

        conv_p = jnp.stack([proj[b * L + L - (CONV_W - 1):(b + 1) * L, 4 * W:6 * W] for b in range(B)])
        conv_s = proj[n_p:, 4 * W:6 * W].reshape(DB, T, 2 * W)[:, T - (CONV_W - 1):]
        per_layer = (
            kvr[:n_p, :lat].reshape(B, L, lat), kvr[:n_p, lat:lat + rope_w].reshape(B, L, rope_w),
            kvr[n_p:, :lat].reshape(DB, T, lat), kvr[n_p:, lat:lat + rope_w].reshape(DB, T, rope_w),
            ret_p, ret_s, c_p, c_s,
            nn_p.reshape(B, n_heads, HEAD_DIM), nn_s.reshape(DB, n_heads, HEAD_DIM),
            mm_p[:, 0, :n_heads], mm_s.reshape(DB, T, HEAD_DIM)[:, T - 1, :n_heads],
            conv_p, conv_s,
        )
        for acc, val in zip(outs, per_layer):
            acc.append(val)

    y_p, y_s = final_norm(h, norm_final[None], n_p)
    return (y_p.reshape(B, L, D), y_s.reshape(DB, T, D)) + tuple(jnp.stack(o) for o in outs)
```

```python
import functools
import math

import numpy as np
import jax
import jax.numpy as jnp
from jax import lax
from jax.experimental import pallas as pl
from jax.experimental.pallas import tpu as pltpu

F32 = jnp.float32
BF16 = jnp.bfloat16

HEAD_DIM = 128
CONV_W = 4
CHUNK = 128
ROPE_BASE = 10000.0
EPS = 1e-6
QK_SCALE = HEAD_DIM ** -0.5
NEG_INF = float("-inf")

MAX_PAGES_IN_VMEM = 64

NT_DIMS = (((1,), (1,)), ((), ()))
TN_DIMS = (((0,), (0,)), ((), ()))


def _params(semantics, vmem_mb):
    return pltpu.CompilerParams(dimension_semantics=semantics, vmem_limit_bytes=vmem_mb << 20)


def _tile(n, target, mult):
    best = None
    for t in range(mult, min(n, target) + 1, mult):
        if n % t == 0:
            best = t
    assert best is not None, (n, target, mult)
    return best


class _Part:
    def __init__(self, body, args, in_specs, out_shape, out_specs, scratch):
        self.body, self.args, self.in_specs = body, list(args), list(in_specs)
        self.out_shape, self.out_specs, self.scratch = list(out_shape), list(out_specs), list(scratch)


def _run_parts(parts, grid, semantics, vmem_mb, name):
    n_in = [len(p.args) for p in parts]
    n_out = [len(p.out_shape) for p in parts]
    n_scr = [len(p.scratch) for p in parts]

    def body(*refs):
        ins, outs, scr = refs[:sum(n_in)], refs[sum(n_in):sum(n_in) + sum(n_out)], refs[sum(n_in) + sum(n_out):]
        i0 = o0 = s0 = 0
        for p, ni, no, ns in zip(parts, n_in, n_out, n_scr):
            p.body(*ins[i0:i0 + ni], *outs[o0:o0 + no], *scr[s0:s0 + ns])
            i0, o0, s0 = i0 + ni, o0 + no, s0 + ns

    res = pl.pallas_call(
        body,
        out_shape=tuple(s for p in parts for s in p.out_shape),
        grid=grid,
        in_specs=[s for p in parts for s in p.in_specs],
        out_specs=tuple(s for p in parts for s in p.out_specs),
        scratch_shapes=[s for p in parts for s in p.scratch],
        compiler_params=_params(semantics, vmem_mb),
        name=name,
    )(*[a for p in parts for a in p.args])
    split, o0 = [], 0
    for no in n_out:
        split.append(res[o0:o0 + no])
        o0 += no
    return split


def _rms(x, g):
    ms = jnp.mean(x * x, axis=-1, keepdims=True)
    return (x * lax.rsqrt(ms + EPS)) * g


def _head_ln(y, g):
    mu = jnp.mean(y, axis=-1, keepdims=True)
    yc = y - mu
    var = jnp.mean(yc * yc, axis=-1, keepdims=True)
    return (yc * lax.rsqrt(var + EPS)) * g


def _norm_matmul_kernel(x_ref, g_ref, wm_ref, wt_ref, o_ref, xn_ref, *, main_tiles):
    j = pl.program_id(1)

    @pl.when(j == 0)
    def _():
        xn_ref[...] = _rms(x_ref[...], g_ref[...]).astype(BF16)

    @pl.when(j < main_tiles)
    def _():
        o_ref[...] = jnp.dot(xn_ref[...], wm_ref[...].astype(BF16), preferred_element_type=F32)

    @pl.when(j >= main_tiles)
    def _():
        o_ref[...] = jnp.dot(xn_ref[...], wt_ref[...], preferred_element_type=F32)


def norm_matmul(x, g, w_main, n_main, w_tail, layer):
    M, D = x.shape
    n_tail = w_tail.shape[2]
    tm = _tile(M, 1088, 16)
    tn = _tile(math.gcd(n_main, n_tail), 512, 128)
    main_tiles = n_main // tn
    return pl.pallas_call(
        functools.partial(_norm_matmul_kernel, main_tiles=main_tiles),
        out_shape=jax.ShapeDtypeStruct((M, n_main + n_tail), F32),
        grid=(M // tm, (n_main + n_tail) // tn),
        in_specs=[
            pl.BlockSpec((tm, D), lambda i, j: (i, 0)),
            pl.BlockSpec((1, D), lambda i, j: (0, 0)),
            pl.BlockSpec((None, D, tn), lambda i, j: (layer, 0, jnp.minimum(j, main_tiles - 1))),
            pl.BlockSpec((None, D, tn), lambda i, j: (layer, 0, jnp.maximum(j - main_tiles, 0))),
        ],
        out_specs=pl.BlockSpec((tm, tn), lambda i, j: (i, j)),
        scratch_shapes=[pltpu.VMEM((tm, D), BF16)],
        compiler_params=_params(("parallel", "arbitrary"), 48),
        name="norm_inproj",
    )(x, g, w_main, w_tail)


def _mlp_kernel(x_ref, g_ref, wu_ref, wd_ref, o_ref, xn_ref):
    @pl.when(pl.program_id(1) == 0)
    def _():
        x = x_ref[...]
        xn_ref[...] = _rms(x, g_ref[...]).astype(BF16)
        o_ref[...] = x

    u = jnp.dot(xn_ref[...], wu_ref[...], preferred_element_type=F32)
    a = jnp.maximum(u, 0.0)
    o_ref[...] += jnp.dot((a * a).astype(BF16), wd_ref[...], preferred_element_type=F32)


def mlp_residual(x, g, w_up, w_down, layer):
    M, D = x.shape
    F = w_up.shape[2]
    tm = _tile(M, 1088, 16)
    tf = _tile(F, 512, 128)
    return pl.pallas_call(
        _mlp_kernel,
        out_shape=jax.ShapeDtypeStruct((M, D), F32),
        grid=(M // tm, F // tf),
        in_specs=[
            pl.BlockSpec((tm, D), lambda i, f: (i, 0), pipeline_mode=pl.Buffered(1)),
            pl.BlockSpec((1, D), lambda i, f: (0, 0)),
            pl.BlockSpec((None, D, tf), lambda i, f: (layer, 0, f)),
            pl.BlockSpec((None, tf, D), lambda i, f: (layer, f, 0)),
        ],
        out_specs=pl.BlockSpec((tm, D), lambda i, f: (i, 0)),
        scratch_shapes=[pltpu.VMEM((tm, D), BF16)],
        compiler_params=_params(("parallel", "arbitrary"), 56),
        name="mlp",
    )(x, g, w_up, w_down)


def _outproj_kernel(x_ref, yrp_ref, yrs_ref, ymp_ref, yms_ref, op_ref, os_ref, wuv_ref, w_ref, out_ref, ymla_ref,
                    *, n_mla, w_ret, w_m, prompt_tiles):
    lat = wuv_ref.shape[1]
    is_prompt = pl.program_id(0) < prompt_tiles
    o_lat = jnp.where(is_prompt, op_ref[...], os_ref[...])
    for h in range(n_mla):
        yh = jnp.dot(o_lat[:, h * lat:(h + 1) * lat], wuv_ref[h], preferred_element_type=F32)
        ymla_ref[:, h * HEAD_DIM:(h + 1) * HEAD_DIM] = yh.astype(BF16)
    acc = x_ref[...]
    acc += jnp.dot(jnp.where(is_prompt, yrp_ref[...], yrs_ref[...]), w_ref[0:w_ret, :], preferred_element_type=F32)
    acc += jnp.dot(jnp.where(is_prompt, ymp_ref[...], yms_ref[...]), w_ref[w_ret:w_ret + w_m, :],
                   preferred_element_type=F32)
    acc += jnp.dot(ymla_ref[...], w_ref[w_ret + w_m:, :], preferred_element_type=F32)
    out_ref[...] = acc


def outproj_residual(x, y_ret, y_m, o_lat, w_uv, w_out, layer):
    M, D = x.shape
    _, n_mla, lat, _ = w_uv.shape
    n_p, n_s = o_lat[0].shape[0], o_lat[1].shape[0]
    w_ret, w_m = y_ret[0].shape[1], y_m[0].shape[1]
    tm = _tile(math.gcd(n_p, n_s), 272, 16)
    prompt_tiles = n_p // tm
    kern = functools.partial(_outproj_kernel, n_mla=n_mla, w_ret=w_ret, w_m=w_m, prompt_tiles=prompt_tiles)

    def pair(width):
        return [pl.BlockSpec((tm, width), lambda i: (jnp.minimum(i, prompt_tiles - 1), 0)),
                pl.BlockSpec((tm, width), lambda i: (jnp.maximum(i - prompt_tiles, 0), 0))]

    return pl.pallas_call(
        kern,
        out_shape=jax.ShapeDtypeStruct((M, D), F32),
        grid=(M // tm,),
        in_specs=[pl.BlockSpec((tm, D), lambda i: (i, 0))] + pair(w_ret) + pair(w_m) + pair(n_mla * lat) + [
            pl.BlockSpec((None,) + w_uv.shape[1:], lambda i: (layer, 0, 0, 0)),
            pl.BlockSpec((None,) + w_out.shape[1:], lambda i: (layer, 0, 0)),
        ],
        out_specs=pl.BlockSpec((tm, D), lambda i: (i, 0)),
        scratch_shapes=[pltpu.VMEM((tm, n_mla * HEAD_DIM), BF16)],
        compiler_params=_params(("parallel",), 48),
        name="outproj",
    )(x, *y_ret, *y_m, *o_lat, w_uv, w_out)


def _final_norm_kernel(x_ref, g_ref, op_ref, os_ref, *, prompt_tiles):
    y = _rms(x_ref[...], g_ref[...])
    i = pl.program_id(0)

    @pl.when(i < prompt_tiles)
    def _():
        op_ref[...] = y

    @pl.when(i >= prompt_tiles)
    def _():
        os_ref[...] = y


def final_norm(x, g, n_p):
    M, D = x.shape
    n_s = M - n_p
    tm = _tile(math.gcd(n_p, n_s), 544, 8)
    prompt_tiles = n_p // tm
    return pl.pallas_call(
        functools.partial(_final_norm_kernel, prompt_tiles=prompt_tiles),
        out_shape=(jax.ShapeDtypeStruct((n_p, D), F32), jax.ShapeDtypeStruct((n_s, D), F32)),
        grid=(M // tm,),
        in_specs=[pl.BlockSpec((tm, D), lambda i: (i, 0)), pl.BlockSpec((1, D), lambda i: (0, 0))],
        out_specs=(pl.BlockSpec((tm, D), lambda i: (jnp.minimum(i, prompt_tiles - 1), 0)),
                   pl.BlockSpec((tm, D), lambda i: (jnp.maximum(i - prompt_tiles, 0), 0))),
        compiler_params=_params(("arbitrary",), 32),
        name="final_norm",
    )(x, g)


def _rope128(x, cos2, sin2):
    return x * cos2 + pltpu.roll(x, HEAD_DIM // 2, 1) * sin2


def _ret_block(q_ref, k_ref, v_ref, g_ref, cos_ref, sin_ref, dm_ref, qd_ref, kd_ref, gn_ref, y_ref,
               load_state, store_state, rowseq, n_heads, n_sub, cdec):
    cos2, sin2 = cos_ref[...], sin_ref[...]
    for h in range(n_heads):
        sl = slice(h * HEAD_DIM, (h + 1) * HEAD_DIM)
        qr = _rope128(q_ref[:, sl], cos2, sin2)
        kr = _rope128(k_ref[:, sl], cos2, sin2) * QK_SCALE
        vb = v_ref[:, sl].astype(BF16)
        a = lax.dot_general(qr.astype(BF16), kr.astype(BF16), NT_DIMS, preferred_element_type=F32) * dm_ref[h]
        o = jnp.dot(a.astype(BF16), vb, preferred_element_type=F32)
        qs = (qr * qd_ref[:, sl]).astype(BF16)
        ks = kr * kd_ref[:, sl]
        for i in range(n_sub):
            s_old = load_state(i, h)
            inter = jnp.dot(qs, s_old.astype(BF16), preferred_element_type=F32)
            if n_sub == 1:
                o = o + inter
                ks_i = ks
            else:
                sel = rowseq == float(i)
                o = o + jnp.where(sel, inter, 0.0)
                ks_i = jnp.where(sel, ks, 0.0)
            kv = lax.dot_general(ks_i.astype(BF16), vb, TN_DIMS, preferred_element_type=F32)
            store_state(i, h, s_old * cdec[h] + kv)
        yn = _head_ln(o, gn_ref[:, sl])
        y_ref[:, sl] = (jax.nn.silu(g_ref[:, sl]) * yn).astype(BF16)


def _ret_prompt_kernel(q_ref, k_ref, v_ref, g_ref, cos_ref, sin_ref, dm_ref, qd_ref, kd_ref, gn_ref,
                       y_ref, so_ref, s_ref, *, n_heads, cdec):
    c = pl.program_id(1)

    @pl.when(c == 0)
    def _():
        s_ref[...] = jnp.zeros_like(s_ref)

    def load_state(i, h):
        return s_ref[h]

    def store_state(i, h, val):
        s_ref[h] = val

    _ret_block(q_ref, k_ref, v_ref, g_ref, cos_ref, sin_ref, dm_ref, qd_ref, kd_ref, gn_ref, y_ref,
               load_state, store_state, None, n_heads, 1, cdec)

    @pl.when(c == pl.num_programs(1) - 1)
    def _():
        so_ref[...] = s_ref[...]


def _ret_sample_kernel(q_ref, k_ref, v_ref, g_ref, cos_ref, sin_ref, dm_ref, qd_ref, kd_ref, gn_ref,
                       rs_ref, s0_ref, y_ref, so_ref, *, n_heads, n_sub, cdec):
    def load_state(i, h):
        return s0_ref[i, h]

    def store_state(i, h, val):
        so_ref[i, h] = val

    _ret_block(q_ref, k_ref, v_ref, g_ref, cos_ref, sin_ref, dm_ref, qd_ref, kd_ref, gn_ref, y_ref,
               load_state, store_state, rs_ref[...], n_heads, n_sub, cdec)


def _ret_decay_tables(n_heads, chunk, n_sub):
    lg = np.log1p(-np.exp2(-5.0 - np.arange(n_heads, dtype=np.float64)))
    idx = np.arange(chunk, dtype=np.float64)
    diff = idx[:, None] - idx[None, :]
    dm1 = np.where(diff[None] >= 0, np.exp(np.maximum(diff, 0.0)[None] * lg[:, None, None]), 0.0)
    rows = n_sub * chunk
    dm = np.zeros((n_heads, rows, rows))
    for i in range(n_sub):
        dm[:, i * chunk:(i + 1) * chunk, i * chunk:(i + 1) * chunk] = dm1
    q_dec = np.exp((idx[:, None] + 1.0) * lg[None, :])
    k_dec = np.exp((chunk - 1.0 - idx)[:, None] * lg[None, :])
    qd = np.tile(np.repeat(q_dec, HEAD_DIM, axis=1), (n_sub, 1))
    kd = np.tile(np.repeat(k_dec, HEAD_DIM, axis=1), (n_sub, 1))
    cdec = tuple(float(np.float32(v)) for v in np.exp(chunk * lg))
    return jnp.asarray(dm, F32), jnp.asarray(qd, F32), jnp.asarray(kd, F32), cdec


def retention_prompt(proj, cos2, sin2, gn, B, L, n_heads):
    W = n_heads * HEAD_DIM
    C = min(CHUNK, L)
    nc = L // C
    dm, qd, kd, cdec = _ret_decay_tables(n_heads, C, 1)
    kern = functools.partial(_ret_prompt_kernel, n_heads=n_heads, cdec=cdec)
    col = lambda j: pl.BlockSpec((C, W), lambda b, c, j=j: (b * nc + c, j))
    const2 = lambda shape: pl.BlockSpec(shape, lambda b, c: (0,) * len(shape))
    return _Part(
        kern,
        args=(proj, proj, proj, proj, cos2, sin2, dm, qd, kd, gn),
        in_specs=[col(0), col(1), col(2), col(3),
                  pl.BlockSpec((C, HEAD_DIM), lambda b, c: (b * nc + c, 0)),
                  pl.BlockSpec((C, HEAD_DIM), lambda b, c: (b * nc + c, 0)),
                  const2(dm.shape), const2(qd.shape), const2(kd.shape), const2((1, W))],
        out_shape=[jax.ShapeDtypeStruct((B * L, W), BF16),
                   jax.ShapeDtypeStruct((B, n_heads, HEAD_DIM, HEAD_DIM), F32)],
        out_specs=[pl.BlockSpec((C, W), lambda b, c: (b * nc + c, 0)),
                   pl.BlockSpec((None, n_heads, HEAD_DIM, HEAD_DIM), lambda b, c: (b, 0, 0, 0))],
        scratch=[pltpu.VMEM((n_heads, HEAD_DIM, HEAD_DIM), F32)])


def retention_sample(proj, cos2, sin2, gn, state, layer, row0, DB, T, n_heads, G):
    W = n_heads * HEAD_DIM
    R = G * T
    rb0 = row0 // R
    dm, qd, kd, cdec = _ret_decay_tables(n_heads, T, G)
    rowseq = jnp.asarray(np.repeat(np.arange(G, dtype=np.float32), T)[:, None] * np.ones((1, HEAD_DIM), np.float32))
    kern = functools.partial(_ret_sample_kernel, n_heads=n_heads, n_sub=G, cdec=cdec)
    col = lambda j: pl.BlockSpec((R, W), lambda g, j=j: (rb0 + g, j))
    const1 = lambda shape: pl.BlockSpec(shape, lambda g: (0,) * len(shape))
    return _Part(
        kern,
        args=(proj, proj, proj, proj, cos2, sin2, dm, qd, kd, gn, rowseq, state),
        in_specs=[col(0), col(1), col(2), col(3),
                  pl.BlockSpec((R, HEAD_DIM), lambda g: (rb0 + g, 0)),
                  pl.BlockSpec((R, HEAD_DIM), lambda g: (rb0 + g, 0)),
                  const1(dm.shape), const1(qd.shape), const1(kd.shape), const1((1, W)),
                  const1(rowseq.shape),
                  pl.BlockSpec((None, G, n_heads, HEAD_DIM, HEAD_DIM), lambda g: (layer, g, 0, 0, 0))],
        out_shape=[jax.ShapeDtypeStruct((DB * T, W), BF16),
                   jax.ShapeDtypeStruct((DB, n_heads, HEAD_DIM, HEAD_DIM), F32)],
        out_specs=[pl.BlockSpec((R, W), lambda g: (g, 0)),
                   pl.BlockSpec((G, n_heads, HEAD_DIM, HEAD_DIM), lambda g: (g, 0, 0, 0))],
        scratch=[])


def _mlstm_block(u, v_ref, o_ref, gates, mprev_ref, mout_ref, nrows, tri, eye, lastm, gn_ref, y_ref,
                 load_c, store_c, store_n, rowseq, n_heads, n_sub, seq_len):
    W = n_heads * HEAD_DIM
    for h in range(n_heads):
        sl = slice(h * HEAD_DIM, (h + 1) * HEAD_DIM)
        q = u[:, h * HEAD_DIM:(h + 1) * HEAD_DIM]
        k = u[:, W + h * HEAD_DIM:W + (h + 1) * HEAD_DIM] * QK_SCALE
        qb, kb = q.astype(BF16), k.astype(BF16)
        vb = v_ref[:, sl].astype(BF16)
        ig_col = gates[:, h:h + 1]
        lf_col = gates[:, n_heads + h:n_heads + h + 1]
        m_col = mprev_ref[:, h:h + 1]
        f_row = jnp.sum(jnp.where(eye, lf_col, 0.0), axis=0, keepdims=True)
        bt_col = jnp.sum(jnp.where(tri, f_row, 0.0), axis=1, keepdims=True)
        bt_row = jnp.sum(jnp.where(eye, bt_col, 0.0), axis=0, keepdims=True)
        it_row = jnp.sum(jnp.where(eye, ig_col, 0.0), axis=0, keepdims=True)
        logw = jnp.where(tri, bt_col - bt_row + it_row, NEG_INF)
        log_inter = bt_col + m_col
        m_i = jnp.maximum(log_inter, jnp.max(logw, axis=1, keepdims=True))
        w = jnp.exp(logw - m_i)
        a = jnp.exp(log_inter - m_i)
        s = lax.dot_general(qb, kb, NT_DIMS, preferred_element_type=F32) * w
        num = jnp.dot(s.astype(BF16), vb, preferred_element_type=F32)
        den = jnp.sum(s, axis=1, keepdims=True) + a * jnp.sum(q * nrows(h), axis=1, keepdims=True)
        if n_sub == 1:
            rows = m_i.shape[0]
            m_last = jnp.broadcast_to(m_i[rows - 1:rows, :], m_i.shape)
            bt_last = jnp.broadcast_to(bt_col[rows - 1:rows, :], m_i.shape)
        else:
            m_row = jnp.sum(jnp.where(eye, m_i, 0.0), axis=0, keepdims=True)
            m_last = jnp.sum(jnp.where(lastm, m_row, 0.0), axis=1, keepdims=True)
            bt_last = jnp.sum(jnp.where(lastm, bt_row, 0.0), axis=1, keepdims=True)
        wl = jnp.exp(bt_last - bt_col + ig_col - m_last)
        al = jnp.exp(bt_last + m_col - m_last)
        kw = k * wl
        inter_sum = None
        for i in range(n_sub):
            c_old = load_c(i, h)
            inter = jnp.dot(qb, c_old.astype(BF16), preferred_element_type=F32)
            if n_sub == 1:
                kw_i = kw
            else:
                sel = rowseq == float(i)
                inter = jnp.where(sel, inter, 0.0)
                kw_i = jnp.where(sel, kw, 0.0)
            inter_sum = inter if inter_sum is None else inter_sum + inter
            al_i = al[i * seq_len:i * seq_len + 1, :]
            kv = lax.dot_general(kw_i.astype(BF16), vb, TN_DIMS, preferred_element_type=F32)
            store_c(i, h, al_i * c_old + kv)
            store_n(i, h, al_i, jnp.sum(kw_i, axis=0, keepdims=True))
        num = num + a * inter_sum
        hh = num / jnp.maximum(jnp.abs(den), jnp.exp(-m_i))
        yn = _head_ln(hh, gn_ref[:, sl])
        y_ref[:, sl] = (jax.nn.sigmoid(o_ref[:, sl]) * yn).astype(BF16)
        mout_ref[:, h:h + 1] = m_last


def _gate_values(g_ref, gb_ref, n_heads):
    pre = g_ref[...] + gb_ref[...]
    lane = lax.broadcasted_iota(jnp.int32, pre.shape, 1)
    lf = jnp.minimum(pre, 0.0) - jnp.log1p(jnp.exp(-jnp.abs(pre)))
    return jnp.where(lane < n_heads, pre, lf)


def _conv_silu(taps, cw_ref, cb_ref):
    acc = cb_ref[...]
    for j in range(CONV_W):
        acc = acc + taps[j] * cw_ref[j:j + 1, :]
    return jax.nn.silu(acc)


def _mlstm_prompt_kernel(x_ref, v_ref, o_ref, g_ref, cw_ref, cb_ref, gb_ref, gn_ref, tri_ref, eye_ref,
                         last_ref, y_ref, co_ref, no_ref, mo_ref,
                         xbuf_ref, c_ref, n_ref, m_ref, *, n_heads):
    c = pl.program_id(1)
    C = x_ref.shape[0]

    @pl.when(c == 0)
    def _():
        xbuf_ref[0:8, :] = jnp.zeros((8, xbuf_ref.shape[1]), F32)
        c_ref[...] = jnp.zeros_like(c_ref)
        n_ref[...] = jnp.zeros_like(n_ref)
        m_ref[...] = jnp.zeros_like(m_ref)

    xbuf_ref[8:8 + C, :] = x_ref[...]
    taps = [xbuf_ref[8 - (CONV_W - 1) + j:8 - (CONV_W - 1) + j + C, :] for j in range(CONV_W)]
    u = _conv_silu(taps, cw_ref, cb_ref)
    xbuf_ref[0:8, :] = x_ref[C - 8:C, :]
    gates = _gate_values(g_ref, gb_ref, n_heads)
    tri, eye, lastm = tri_ref[...] > 0.5, eye_ref[...] > 0.5, last_ref[...] > 0.5

    def load_c(i, h):
        return c_ref[h]

    def store_c(i, h, val):
        c_ref[h] = val

    def store_n(i, h, al_i, ksum):
        n_ref[h] = al_i * n_ref[h] + ksum

    def nrows(h):
        return n_ref[h]

    _mlstm_block(u, v_ref, o_ref, gates, m_ref, m_ref, nrows, tri, eye, lastm, gn_ref, y_ref,
                 load_c, store_c, store_n, None, n_heads, 1, C)

    @pl.when(c == pl.num_programs(1) - 1)
    def _():
        co_ref[...] = c_ref[...]
        no_ref[...] = n_ref[...]
        mo_ref[...] = m_ref[0:8, :]


def _mlstm_sample_kernel(x_ref, v_ref, o_ref, g_ref, cw_ref, cb_ref, gb_ref, gn_ref, tri_ref, eye_ref,
                         last_ref, rs_ref, rt_ref, st_ref, c0_ref, n0_ref, nr_ref, m0_ref,
                         y_ref, co_ref, no_ref, mo_ref, *, n_heads, n_sub, seq_len):
    R = x_ref.shape[0]
    x = x_ref[...]
    st = st_ref[...]
    tpos = rt_ref[:, 0:1]
    taps = []
    for j in range(CONV_W):
        s = CONV_W - 1 - j
        if s == 0:
            taps.append(x)
        else:
            cur = pltpu.roll(x, s, 0)
            old = pltpu.roll(st, (s - seq_len) % R, 0)
            taps.append(jnp.where(tpos >= float(s), cur, old))
    u = _conv_silu(taps, cw_ref, cb_ref)
    gates = _gate_values(g_ref, gb_ref, n_heads)
    tri, eye, lastm = tri_ref[...] > 0.5, eye_ref[...] > 0.5, last_ref[...] > 0.5
    mo_ref[...] = jnp.zeros_like(mo_ref)

    def load_c(i, h):
        return c0_ref[i, h]

    def store_c(i, h, val):
        co_ref[i, h] = val

    def store_n(i, h, al_i, ksum):
        sl = slice(h * HEAD_DIM, (h + 1) * HEAD_DIM)
        no_ref[i:i + 1, sl] = al_i * n0_ref[i:i + 1, sl] + ksum

    def nrows(h):
        return nr_ref[:, h * HEAD_DIM:(h + 1) * HEAD_DIM]

    _mlstm_block(u, v_ref, o_ref, gates, m0_ref, mo_ref, nrows, tri, eye, lastm, gn_ref, y_ref,
                 load_c, store_c, store_n, rs_ref[...], n_heads, n_sub, seq_len)


def _seq_masks(n_sub, seq_len):
    rows = n_sub * seq_len
    r = np.arange(rows)
    seq, t = r // seq_len, r % seq_len
    same = seq[:, None] == seq[None, :]
    tri = (same & (t[None, :] <= t[:, None])).astype(np.float32)
    eye = np.eye(rows, dtype=np.float32)
    last = (r[None, :] == (seq[:, None] * seq_len + seq_len - 1)).astype(np.float32)
    return jnp.asarray(tri), jnp.asarray(eye), jnp.asarray(last)


def mlstm_prompt(proj, conv_w, conv_b, gate_b, gn, B, L, n_heads, gate_blk):
    W = n_heads * HEAD_DIM
    C = min(CHUNK, L)
    nc = L // C
    tri, eye, last = _seq_masks(1, C)
    kern = functools.partial(_mlstm_prompt_kernel, n_heads=n_heads)
    const2 = lambda shape: pl.BlockSpec(shape, lambda b, c: (0,) * len(shape))
    return _Part(
        kern,
        args=(proj, proj, proj, proj, conv_w, conv_b, gate_b, gn, tri, eye, last),
        in_specs=[pl.BlockSpec((C, 2 * W), lambda b, c: (b * nc + c, 2)),
                  pl.BlockSpec((C, W), lambda b, c: (b * nc + c, 6)),
                  pl.BlockSpec((C, W), lambda b, c: (b * nc + c, 7)),
                  pl.BlockSpec((C, HEAD_DIM), lambda b, c: (b * nc + c, gate_blk)),
                  const2((CONV_W, 2 * W)), const2((1, 2 * W)), const2((1, HEAD_DIM)), const2((1, W)),
                  const2(tri.shape), const2(eye.shape), const2(last.shape)],
        out_shape=[jax.ShapeDtypeStruct((B * L, W), BF16),
                   jax.ShapeDtypeStruct((B, n_heads, HEAD_DIM, HEAD_DIM), F32),
                   jax.ShapeDtypeStruct((B, n_heads, 1, HEAD_DIM), F32),
                   jax.ShapeDtypeStruct((B, 8, HEAD_DIM), F32)],
        out_specs=[pl.BlockSpec((C, W), lambda b, c: (b * nc + c, 0)),
                   pl.BlockSpec((None, n_heads, HEAD_DIM, HEAD_DIM), lambda b, c: (b, 0, 0, 0)),
                   pl.BlockSpec((None, n_heads, 1, HEAD_DIM), lambda b, c: (b, 0, 0, 0)),
                   pl.BlockSpec((None, 8, HEAD_DIM), lambda b, c: (b, 0, 0))],
        scratch=[pltpu.VMEM((8 + C, 2 * W), F32),
                 pltpu.VMEM((n_heads, HEAD_DIM, HEAD_DIM), F32),
                 pltpu.VMEM((n_heads, 1, HEAD_DIM), F32),
                 pltpu.VMEM((C, HEAD_DIM), F32)])


def mlstm_sample(proj, conv_w, conv_b, gate_b, gn, conv_rows, c_state, n_state, n_rows, m_rows,
                 layer, row0, DB, T, n_heads, gate_blk, G):
    W = n_heads * HEAD_DIM
    R = G * T
    rb0 = row0 // R
    tri, eye, last = _seq_masks(G, T)
    ones = np.ones((1, HEAD_DIM), np.float32)
    rowseq = jnp.asarray(np.repeat(np.arange(G, dtype=np.float32), T)[:, None] * ones)
    rowt = jnp.asarray(np.tile(np.arange(T, dtype=np.float32), G)[:, None] * ones)
    kern = functools.partial(_mlstm_sample_kernel, n_heads=n_heads, n_sub=G, seq_len=T)
    const1 = lambda shape: pl.BlockSpec(shape, lambda g: (0,) * len(shape))
    return _Part(
        kern,
        args=(proj, proj, proj, proj, conv_w, conv_b, gate_b, gn, tri, eye, last, rowseq, rowt,
              conv_rows, c_state, n_state, n_rows, m_rows),
        out_shape=[jax.ShapeDtypeStruct((DB * T, W), BF16),
                   jax.ShapeDtypeStruct((DB, n_heads, HEAD_DIM, HEAD_DIM), F32),
                   jax.ShapeDtypeStruct((DB, W), F32),
                   jax.ShapeDtypeStruct((DB * T, HEAD_DIM), F32)],
        in_specs=[pl.BlockSpec((R, 2 * W), lambda g: (rb0 + g, 2)),
                  pl.BlockSpec((R, W), lambda g: (rb0 + g, 6)),
                  pl.BlockSpec((R, W), lambda g: (rb0 + g, 7)),
                  pl.BlockSpec((R, HEAD_DIM), lambda g: (rb0 + g, gate_blk)),
                  const1((CONV_W, 2 * W)), const1((1, 2 * W)), const1((1, HEAD_DIM)), const1((1, W)),
                  const1(tri.shape), const1(eye.shape), const1(last.shape),
                  const1(rowseq.shape), const1(rowt.shape),
                  pl.BlockSpec((None, R, 2 * W), lambda g: (layer, g, 0)),
                  pl.BlockSpec((None, G, n_heads, HEAD_DIM, HEAD_DIM), lambda g: (layer, g, 0, 0, 0)),
                  pl.BlockSpec((None, G, W), lambda g: (layer, g, 0)),
                  pl.BlockSpec((None, R, W), lambda g: (layer, g, 0)),
                  pl.BlockSpec((None, R, HEAD_DIM), lambda g: (layer, g, 0))],
        out_specs=[pl.BlockSpec((R, W), lambda g: (g, 0)),
                   pl.BlockSpec((G, n_heads, HEAD_DIM, HEAD_DIM), lambda g: (g, 0, 0, 0)),
                   pl.BlockSpec((G, W), lambda g: (g, 0)),
                   pl.BlockSpec((R, HEAD_DIM), lambda g: (g, 0))],
        scratch=[])


def _rope64x2(x, cos4, sin_lo, sin_hi):
    return x * cos4 + pltpu.roll(x, 96, 1) * sin_lo + pltpu.roll(x, 32, 1) * sin_hi


def _mla_prep_kernel(qn_ref, qp_ref, kv_ref, cos_ref, slo_ref, shi_ref, kvn_ref, wuk_ref,
                     kvr_ref, kb_ref, q_ref, *, n_heads, lat):
    cos4, slo, shi = cos_ref[...], slo_ref[...], shi_ref[...]
    c_new = _rms(kv_ref[:, 0:lat], kvn_ref[...])
    kr = _rope64x2(kv_ref[:, lat:lat + HEAD_DIM], cos4, slo, shi)
    kvr_ref[:, 0:lat] = c_new
    kvr_ref[:, lat:lat + HEAD_DIM] = kr
    kb_ref[:, 0:lat] = c_new.astype(BF16)
    kb_ref[:, lat:lat + HEAD_DIM] = (kr + pltpu.roll(kr, HEAD_DIM // 2, 1)).astype(BF16)
    qw = lat + HEAD_DIM
    lane = lax.broadcasted_iota(jnp.int32, (qp_ref.shape[0], HEAD_DIM), 1)
    for p in range(n_heads // 2):
        pe = _rope64x2(qp_ref[:, p * HEAD_DIM:(p + 1) * HEAD_DIM], cos4, slo, shi)
        for half in range(2):
            h = 2 * p + half
            q_lat = jnp.dot(qn_ref[:, h * HEAD_DIM:(h + 1) * HEAD_DIM].astype(BF16), wuk_ref[h],
                            preferred_element_type=F32)
            q_ref[:, h * qw:h * qw + lat] = q_lat.astype(BF16)
            keep = (lane < HEAD_DIM // 2) if half == 0 else (lane >= HEAD_DIM // 2)
            q_ref[:, h * qw + lat:(h + 1) * qw] = jnp.where(keep, pe, 0.0).astype(BF16)


def mla_prep(proj, cos4, sin_lo, sin_hi, kv_norm, w_uk_t, n_heads, lat):
    M = proj.shape[0]
    tm = _tile(M, 272, 16)
    qw = lat + HEAD_DIM
    kern = functools.partial(_mla_prep_kernel, n_heads=n_heads, lat=lat)
    row = lambda w, j: pl.BlockSpec((tm, w), lambda i, j=j: (i, j))
    const1 = lambda shape: pl.BlockSpec(shape, lambda i: (0,) * len(shape))
    nope_w, pe_w = n_heads * HEAD_DIM, n_heads * HEAD_DIM // 2
    return pl.pallas_call(
        kern,
        out_shape=(jax.ShapeDtypeStruct((M, qw), F32),
                   jax.ShapeDtypeStruct((M, qw), BF16),
                   jax.ShapeDtypeStruct((M, n_heads * qw), BF16)),
        grid=(M // tm,),
        in_specs=[row(nope_w, 4), row(pe_w, 10), row(qw, 15),
                  row(HEAD_DIM, 0), row(HEAD_DIM, 0), row(HEAD_DIM, 0),
                  const1((1, lat)), const1(w_uk_t.shape)],
        out_specs=(row(qw, 0), row(qw, 0), row(n_heads * qw, 0)),
        compiler_params=_params(("parallel",), 32),
        name="mla_prep",
    )(proj, proj, proj, cos4, sin_lo, sin_hi, kv_norm, w_uk_t)


def _mla_prompt_kernel(q_ref, kb_ref, o_ref, qt_ref, m_ref, l_ref, acc_ref, *, n_heads, lat, tq, scale):
    i = pl.program_id(1)
    qw = lat + HEAD_DIM
    cols = n_heads * tq
    for h in range(n_heads):
        qt_ref[:, h * tq:(h + 1) * tq] = q_ref[:, h * qw:(h + 1) * qw].astype(F32).T.astype(BF16)
    m_ref[...] = jnp.full(m_ref.shape, NEG_INF, F32)
    l_ref[...] = jnp.zeros_like(l_ref)
    acc_ref[...] = jnp.zeros_like(acc_ref)

    def chunk(j, masked):
        kblk = kb_ref[pl.ds(pl.multiple_of(j * tq, tq), tq), :]
        s = jnp.dot(kblk, qt_ref[...], preferred_element_type=F32) * scale
        if masked:
            kloc = lax.broadcasted_iota(jnp.int32, (tq, cols), 0)
            qloc = lax.broadcasted_iota(jnp.int32, (tq, cols), 1) & (tq - 1)
            s = jnp.where(kloc <= qloc, s, NEG_INF)
        m_old = m_ref[...]
        m_new = jnp.maximum(m_old, jnp.max(s, axis=0, keepdims=True))
        alpha = jnp.exp(m_old - m_new)
        p = jnp.exp(s - m_new)
        l_ref[...] = alpha * l_ref[...] + jnp.sum(p, axis=0, keepdims=True)
        m_ref[...] = m_new
        vt = kblk[:, 0:lat].astype(F32).T.astype(BF16)
        acc_ref[...] = alpha * acc_ref[...] + jnp.dot(vt, p.astype(BF16), preferred_element_type=F32)

    def body(j, carry):
        chunk(j, False)
        return carry

    lax.fori_loop(0, i, body, 0)
    chunk(i, True)
    ot = acc_ref[...] / l_ref[...]
    for h in range(n_heads):
        o_ref[:, h * lat:(h + 1) * lat] = ot[:, h * tq:(h + 1) * tq].T.astype(BF16)


def mla_prompt(q, kb, B, L, n_heads, lat, scale):
    qw = lat + HEAD_DIM
    tq = _tile(L, 256, 128)
    assert tq & (tq - 1) == 0
    nq = L // tq
    cols = n_heads * tq
    kern = functools.partial(_mla_prompt_kernel, n_heads=n_heads, lat=lat, tq=tq, scale=scale)
    return pl.pallas_call(
        kern,
        out_shape=jax.ShapeDtypeStruct((B * L, n_heads * lat), BF16),
        grid=(B, nq),
        in_specs=[pl.BlockSpec((tq, n_heads * qw), lambda b, i: (b * nq + i, 0)),
                  pl.BlockSpec((L, qw), lambda b, i: (b, 0))],
        out_specs=pl.BlockSpec((tq, n_heads * lat), lambda b, i: (b * nq + i, 0)),
        scratch_shapes=[pltpu.VMEM((qw, cols), BF16),
                        pltpu.VMEM((1, cols), F32),
                        pltpu.VMEM((1, cols), F32),
                        pltpu.VMEM((lat, cols), F32)],
        compiler_params=_params(("parallel", "arbitrary"), 48),
        name="mla_prompt",
    )(q, kb)


def _mla_sample_kernel(pt_ref, q_ref, kn_ref, kv_hbm, krt_hbm, o_ref, kvbuf, krbuf, cbuf_ref, rbuf_ref, sem,
                       *, layer, n_pages, lat, n_heads, scale):
    b = pl.program_id(0)
    nb = pl.num_programs(0)
    slot = b & 1
    page = kvbuf.shape[2]
    rope_w = krbuf.shape[2]

    def page_copies(seq, dst_slot):
        copies = []
        for p in range(n_pages):
            pg = pt_ref[seq * n_pages + p]
            copies.append(pltpu.make_async_copy(kv_hbm.at[layer, pg], kvbuf.at[dst_slot, p], sem.at[0, dst_slot]))
            copies.append(pltpu.make_async_copy(krt_hbm.at[layer, pg], krbuf.at[dst_slot, p], sem.at[1, dst_slot]))
        return copies

    @pl.when(b == 0)
    def _():
        for cp in page_copies(0, 0):
            cp.start()

    for cp in page_copies(b, slot):
        cp.wait()
    nxt = jnp.minimum(b + 1, nb - 1)
    for cp in page_copies(nxt, 1 - slot):
        cp.start()

    for p in range(n_pages):
        cbuf_ref[p * page:(p + 1) * page, :] = kvbuf[slot, p].astype(BF16)
        rbuf_ref[:, p * page:(p + 1) * page] = krbuf[slot, p].astype(BF16)
    q = q_ref[...]
    cb = cbuf_ref[...]
    q_pe = q[:, lat:lat + rope_w] + q[:, lat + rope_w:]
    s = (lax.dot_general(q[:, 0:lat], cb, NT_DIMS, preferred_element_type=F32)
         + jnp.dot(q_pe, rbuf_ref[...], preferred_element_type=F32)) * scale
    kn = kn_ref[...]
    sn = lax.dot_general(q, kn, NT_DIMS, preferred_element_type=F32) * scale
    t_q = jnp.right_shift(lax.broadcasted_iota(jnp.int32, sn.shape, 0), n_heads.bit_length() - 1)
    t_k = lax.broadcasted_iota(jnp.int32, sn.shape, 1)
    sn = jnp.where(t_k <= t_q, sn, NEG_INF)
    m = jnp.maximum(jnp.max(s, axis=1, keepdims=True), jnp.max(sn, axis=1, keepdims=True))
    pc = jnp.exp(s - m)
    pn = jnp.exp(sn - m)
    denom = jnp.sum(pc, axis=1, keepdims=True) + jnp.sum(pn, axis=1, keepdims=True)
    acc = (jnp.dot(pc.astype(BF16), cb, preferred_element_type=F32)
           + jnp.dot(pn.astype(BF16), kn[:, 0:lat], preferred_element_type=F32))
    o_ref[...] = (acc / denom).astype(BF16)

    @pl.when(b == nb - 1)
    def _():
        for cp in page_copies(nxt, 1 - slot):
            cp.wait()


def mla_sample(q_s, kb_new, cache_kv, cache_krt, page_table, layer, n_heads, lat, scale):
    DB, rows, qw = q_s.shape
    n_pages = page_table.shape[1]
    page = cache_kv.shape[2]
    rope_w = cache_krt.shape[2]
    seq_len = rows // n_heads
    assert seq_len <= page and n_heads & (n_heads - 1) == 0 and 2 * rope_w == HEAD_DIM
    assert n_pages <= MAX_PAGES_IN_VMEM, "a sequence's cache pages are held in VMEM twice (two slots)"
    kern = functools.partial(_mla_sample_kernel, layer=layer, n_pages=n_pages, lat=lat, n_heads=n_heads,
                             scale=scale)
    grid_spec = pltpu.PrefetchScalarGridSpec(
        num_scalar_prefetch=1,
        grid=(DB,),
        in_specs=[pl.BlockSpec((None, rows, qw), lambda b, pt: (b, 0, 0)),
                  pl.BlockSpec((None, page, qw), lambda b, pt: (b, 0, 0)),
                  pl.BlockSpec(memory_space=pl.ANY),
                  pl.BlockSpec(memory_space=pl.ANY)],
        out_specs=pl.BlockSpec((None, rows, lat), lambda b, pt: (b, 0, 0)),
        scratch_shapes=[pltpu.VMEM((2, n_pages, page, lat), F32),
                        pltpu.VMEM((2, n_pages, rope_w, page), F32),
                        pltpu.VMEM((n_pages * page, lat), BF16),
                        pltpu.VMEM((rope_w, n_pages * page), BF16),
                        pltpu.SemaphoreType.DMA((2, 2))])
    return pl.pallas_call(
        kern,
        out_shape=jax.ShapeDtypeStruct((DB, rows, lat), BF16),
        grid_spec=grid_spec,
        compiler_params=_params(("arbitrary",), 56),
        name="mla_sample",
    )(page_table.reshape(-1), q_s, kb_new, cache_kv, cache_krt)


def _rope_tables(pos, dim):
    inv = ROPE_BASE ** (-jnp.arange(0, dim, 2, dtype=F32) / dim)
    ang = pos.astype(F32)[:, None] * inv[None, :]
    return jnp.cos(ang), jnp.sin(ang)


def _relayout_w_in_tail(w_in, n_main, n_mla, nope, rope_w, lat):
    depth, D, _ = w_in.shape
    n_gate = 2 * (D // (4 * HEAD_DIM))
    w = w_in[..., n_main:].astype(BF16)
    gates = w[..., :n_gate]
    mlq = w[..., n_gate:n_gate + n_mla * (nope + rope_w)].reshape(depth, D, n_mla, nope + rope_w)
    q_nope = mlq[..., :nope].reshape(depth, D, n_mla * nope)
    q_pe = mlq[..., nope:].reshape(depth, D, n_mla * rope_w)
    k0 = n_gate + n_mla * (nope + rope_w)
    mkv = w[..., k0:k0 + lat + rope_w]
    zeros = lambda n: jnp.zeros((depth, D, n), BF16)
    return jnp.concatenate([q_nope, q_pe, gates, zeros(HEAD_DIM - n_gate), mkv, zeros(HEAD_DIM - rope_w)], axis=-1)


def kernel(x_prompt, x_sample, cache_mla_kv, cache_mla_kr, page_table, state_ret, state_mlstm_C, state_mlstm_n, state_mlstm_m, state_mlstm_conv, norm_attn, norm_mlp, norm_final, w_in, ret_gn, mlstm_conv_w, mlstm_conv_b, mlstm_gate_b, mlstm_gn, mla_kv_norm, w_uk, w_uv, w_out, w_up, w_down):
    B, L, D = x_prompt.shape
    DB, T, _ = x_sample.shape
    depth = w_in.shape[0]
    n_heads = D // (4 * HEAD_DIM)
    W = n_heads * HEAD_DIM
    n_mla = D // (2 * HEAD_DIM)
    lat = mla_kv_norm.shape[1]
    rope_w = cache_mla_kr.shape[3]
    page = cache_mla_kv.shape[2]
    past_len = page_table.shape[1] * page
    assert D == 2048 and lat == 2 * HEAD_DIM and rope_w == HEAD_DIM // 2 and T >= CONV_W - 1 and L % min(CHUNK, L) == 0
    mla_scale = (HEAD_DIM + rope_w) ** -0.5
    G = 8
    n_p, n_s = B * L, DB * T
    gate_blk = (4 * W + 4 * W + n_mla * HEAD_DIM + n_mla * rope_w) // HEAD_DIM
    assert n_p % (G * T) == 0 and DB % G == 0

    pos = jnp.concatenate([jnp.tile(jnp.arange(L, dtype=jnp.int32), B),
                           jnp.tile(past_len + jnp.arange(T, dtype=jnp.int32), DB)])
    cos_h, sin_h = _rope_tables(pos, HEAD_DIM)
    cos2 = jnp.concatenate([cos_h, cos_h], axis=1)
    sin2 = jnp.concatenate([-sin_h, sin_h], axis=1)
    cos_r, sin_r = _rope_tables(pos, rope_w)
    zero_r = jnp.zeros_like(sin_r)
    cos4 = jnp.concatenate([cos_r] * 4, axis=1)
    sin_lo = jnp.concatenate([-sin_r, zero_r, -sin_r, zero_r], axis=1)
    sin_hi = jnp.concatenate([zero_r, sin_r, zero_r, sin_r], axis=1)

    n_main = 8 * W
    w_in_tail = _relayout_w_in_tail(w_in, n_main, n_mla, HEAD_DIM, rope_w, lat)
    w_out_b, w_up_b, w_down_b = w_out.astype(BF16), w_up.astype(BF16), w_down.astype(BF16)
    w_uk_t = jnp.transpose(w_uk, (0, 2, 3, 1)).astype(BF16)
    w_uv_t = jnp.transpose(w_uv, (0, 2, 1, 3)).astype(BF16)
    gate_b = jnp.pad(mlstm_gate_b, ((0, 0), (0, HEAD_DIM - mlstm_gate_b.shape[1])))

    conv_rows = jnp.pad(state_mlstm_conv, ((0, 0), (0, 0), (T - (CONV_W - 1), 0), (0, 0))).reshape(depth, n_s, 2 * W)
    n_state = state_mlstm_n.reshape(depth, DB, W)
    n_rows = jnp.repeat(n_state, T, axis=1)
    m_rows = jnp.repeat(jnp.pad(state_mlstm_m, ((0, 0), (0, 0), (0, HEAD_DIM - n_heads))), T, axis=1)

    cache_krt = jnp.swapaxes(cache_mla_kr, 2, 3)

    h = jnp.concatenate([x_prompt.reshape(n_p, D), x_sample.reshape(n_s, D)], axis=0)
    outs = [[] for _ in range(14)]
    for l in range(depth):
        proj = norm_matmul(h, norm_attn[l][None], w_in, n_main, w_in_tail, l)

        gn_r = ret_gn[l][None]
        cw, cb, gb, gn_m = mlstm_conv_w[l], mlstm_conv_b[l][None], gate_b[l][None], mlstm_gn[l][None]
        (yr_p, ret_p), (ym_p, c_p, nn_p, mm_p) = _run_parts(
            [retention_prompt(proj, cos2, sin2, gn_r, B, L, n_heads),
             mlstm_prompt(proj, cw, cb, gb, gn_m, B, L, n_heads, gate_blk)],
            (B, L // min(CHUNK, L)), ("parallel", "arbitrary"), 40, "recurrent_prompt")
        (yr_s, ret_s), (ym_s, c_s, nn_s, mm_s) = _run_parts(
            [retention_sample(proj, cos2, sin2, gn_r, state_ret, l, n_p, DB, T, n_heads, G),
             mlstm_sample(proj, cw, cb, gb, gn_m, conv_rows, state_mlstm_C, n_state,
                          n_rows, m_rows, l, n_p, DB, T, n_heads, gate_blk, G)],
            (DB // G,), ("parallel",), 40, "recurrent_sample")

        kvr, kb, q = mla_prep(proj, cos4, sin_lo, sin_hi, mla_kv_norm[l][None], w_uk_t[l], n_mla, lat)
        o_p = mla_prompt(q, kb, B, L, n_mla, lat, mla_scale)
        q_s = q[n_p:].reshape(DB, T * n_mla, lat + HEAD_DIM)
        kb_new = jnp.pad(kb[n_p:].reshape(DB, T, lat + HEAD_DIM), ((0, 0), (0, page - T), (0, 0)))
        o_s = mla_sample(q_s, kb_new, cache_mla_kv, cache_krt, page_table, l, n_mla, lat, mla_scale)

        h = outproj_residual(h, (yr_p, yr_s), (ym_p, ym_s), (o_p, o_s.reshape(n_s, n_mla * lat)),
                             w_uv_t, w_out_b, l)
        h = mlp_residual(h, norm_mlp[l][None], w_up_b, w_down_b, l)

        conv_p = jnp.stack([proj[b * L + L - (CONV_W - 1):(b + 1) * L, 4 * W:6 * W] for b in range(B)])
        conv_s = proj[n_p:, 4 * W:6 * W].reshape(DB, T, 2 * W)[:, T - (CONV_W - 1):]
        per_layer = (
            kvr[:n_p, :lat].reshape(B, L, lat), kvr[:n_p, lat:lat + rope_w].reshape(B, L, rope_w),
            kvr[n_p:, :lat].reshape(DB, T, lat), kvr[n_p:, lat:lat + rope_w].reshape(DB, T, rope_w),
            ret_p, ret_s, c_p, c_s,
            nn_p.reshape(B, n_heads, HEAD_DIM), nn_s.reshape(DB, n_heads, HEAD_DIM),
            mm_p[:, 0, :n_heads], mm_s.reshape(DB, T, HEAD_DIM)[:, T - 1, :n_heads],
            conv_p, conv_s,
        )
        for acc, val in zip(outs, per_layer):
            acc.append(val)

    y_p, y_s = final_norm(h, norm_final[None], n_p)
    return (y_p.reshape(B, L, D), y_s.reshape(DB, T, D)) + tuple(jnp.stack(o) for o in outs)
```

```python
import functools
import math

import numpy as np
import jax
import jax.numpy as jnp
from jax import lax
from jax.experimental import pallas as pl
from jax.experimental.pallas import tpu as pltpu

F32 = jnp.float32
BF16 = jnp.bfloat16

HEAD_DIM = 128
CONV_W = 4
CHUNK = 128
ROPE_BASE = 10000.0
EPS = 1e-6
QK_SCALE = HEAD_DIM ** -0.5
NEG_INF = float("-inf")

MAX_PAGES_IN_VMEM = 64

NT_DIMS = (((1,), (1,)), ((), ()))
TN_DIMS = (((0,), (0,)), ((), ()))


def _params(semantics, vmem_mb):
    return pltpu.CompilerParams(dimension_semantics=semantics, vmem_limit_bytes=vmem_mb << 20)


def _tile(n, target, mult):
    best = None
    for t in range(mult, min(n, target) + 1, mult):
        if n % t == 0:
            best = t
    assert best is not None, (n, target, mult)
    return best


class _Part:
    def __init__(self, body, args, in_specs, out_shape, out_specs, scratch):
        self.body, self.args, self.in_specs = body, list(args), list(in_specs)
        self.out_shape, self.out_specs, self.scratch = list(out_shape), list(out_specs), list(scratch)


def _run_parts(parts, grid, semantics, vmem_mb, name):
    n_in = [len(p.args) for p in parts]
    n_out = [len(p.out_shape) for p in parts]
    n_scr = [len(p.scratch) for p in parts]

    def body(*refs):
        ins, outs, scr = refs[:sum(n_in)], refs[sum(n_in):sum(n_in) + sum(n_out)], refs[sum(n_in) + sum(n_out):]
        i0 = o0 = s0 = 0
        for p, ni, no, ns in zip(parts, n_in, n_out, n_scr):
            p.body(*ins[i0:i0 + ni], *outs[o0:o0 + no], *scr[s0:s0 + ns])
            i0, o0, s0 = i0 + ni, o0 + no, s0 + ns

    res = pl.pallas_call(
        body,
        out_shape=tuple(s for p in parts for s in p.out_shape),
        grid=grid,
        in_specs=[s for p in parts for s in p.in_specs],
        out_specs=tuple(s for p in parts for s in p.out_specs),
        scratch_shapes=[s for p in parts for s in p.scratch],
        compiler_params=_params(semantics, vmem_mb),
        name=name,
    )(*[a for p in parts for a in p.args])
    split, o0 = [], 0
    for no in n_out:
        split.append(res[o0:o0 + no])
        o0 += no
    return split


def _rms(x, g):
    ms = jnp.mean(x * x, axis=-1, keepdims=True)
    return (x * lax.rsqrt(ms + EPS)) * g


def _head_ln(y, g):
    mu = jnp.mean(y, axis=-1, keepdims=True)
    yc = y - mu
    var = jnp.mean(yc * yc, axis=-1, keepdims=True)
    return (yc * lax.rsqrt(var + EPS)) * g


def _norm_matmul_kernel(x_ref, g_ref, w_ref, o_ref, xn_ref):
    @pl.when(pl.program_id(1) == 0)
    def _():
        xn_ref[...] = _rms(x_ref[...], g_ref[...]).astype(BF16)

    o_ref[...] = jnp.dot(xn_ref[...], w_ref[...], preferred_element_type=F32)


def norm_matmul(x, g, w, layer):
    M, D = x.shape
    N = w.shape[2]
    tm = _tile(M, 1088, 16)
    tn = _tile(N, 1024, 128)
    return pl.pallas_call(
        _norm_matmul_kernel,
        out_shape=jax.ShapeDtypeStruct((M, N), F32),
        grid=(M // tm, N // tn),
        in_specs=[
            pl.BlockSpec((tm, D), lambda i, j: (i, 0)),
            pl.BlockSpec((1, D), lambda i, j: (0, 0)),
            pl.BlockSpec((None, D, tn), lambda i, j: (layer, 0, j)),
        ],
        out_specs=pl.BlockSpec((tm, tn), lambda i, j: (i, j)),
        scratch_shapes=[pltpu.VMEM((tm, D), BF16)],
        compiler_params=_params(("parallel", "arbitrary"), 48),
        name="norm_inproj",
    )(x, g, w)


def _mlp_kernel(x_ref, g_ref, wu_ref, wd_ref, o_ref, xn_ref):
    @pl.when(pl.program_id(1) == 0)
    def _():
        x = x_ref[...]
        xn_ref[...] = _rms(x, g_ref[...]).astype(BF16)
        o_ref[...] = x

    u = jnp.dot(xn_ref[...], wu_ref[...], preferred_element_type=F32)
    a = jnp.maximum(u, 0.0)
    o_ref[...] += jnp.dot((a * a).astype(BF16), wd_ref[...], preferred_element_type=F32)


def mlp_residual(x, g, w_up, w_down, layer):
    M, D = x.shape
    F = w_up.shape[2]
    tm = _tile(M, 1088, 16)
    tf = _tile(F, 1024, 128)
    return pl.pallas_call(
        _mlp_kernel,
        out_shape=jax.ShapeDtypeStruct((M, D), F32),
        grid=(M // tm, F // tf),
        in_specs=[
            pl.BlockSpec((tm, D), lambda i, f: (i, 0), pipeline_mode=pl.Buffered(1)),
            pl.BlockSpec((1, D), lambda i, f: (0, 0)),
            pl.BlockSpec((None, D, tf), lambda i, f: (layer, 0, f)),
            pl.BlockSpec((None, tf, D), lambda i, f: (layer, f, 0)),
        ],
        out_specs=pl.BlockSpec((tm, D), lambda i, f: (i, 0), pipeline_mode=pl.Buffered(1)),
        scratch_shapes=[pltpu.VMEM((tm, D), BF16)],
        compiler_params=_params(("parallel", "arbitrary"), 56),
        name="mlp",
    )(x, g, w_up, w_down)


def _outproj_kernel(x_ref, yrp_ref, yrs_ref, ymp_ref, yms_ref, op_ref, os_ref, wuv_ref, w_ref, out_ref, ymla_ref,
                    *, n_mla, w_ret, w_m, prompt_tiles):
    lat = wuv_ref.shape[1]
    is_prompt = pl.program_id(0) < prompt_tiles
    o_lat = jnp.where(is_prompt, op_ref[...], os_ref[...])
    for h in range(n_mla):
        yh = jnp.dot(o_lat[:, h * lat:(h + 1) * lat], wuv_ref[h], preferred_element_type=F32)
        ymla_ref[:, h * HEAD_DIM:(h + 1) * HEAD_DIM] = yh.astype(BF16)
    acc = x_ref[...]
    acc += jnp.dot(jnp.where(is_prompt, yrp_ref[...], yrs_ref[...]), w_ref[0:w_ret, :], preferred_element_type=F32)
    acc += jnp.dot(jnp.where(is_prompt, ymp_ref[...], yms_ref[...]), w_ref[w_ret:w_ret + w_m, :],
                   preferred_element_type=F32)
    acc += jnp.dot(ymla_ref[...], w_ref[w_ret + w_m:, :], preferred_element_type=F32)
    out_ref[...] = acc


def outproj_residual(x, y_ret, y_m, o_lat, w_uv, w_out, layer):
    M, D = x.shape
    _, n_mla, lat, _ = w_uv.shape
    n_p, n_s = o_lat[0].shape[0], o_lat[1].shape[0]
    w_ret, w_m = y_ret[0].shape[1], y_m[0].shape[1]
    tm = _tile(math.gcd(n_p, n_s), 272, 16)
    prompt_tiles = n_p // tm
    kern = functools.partial(_outproj_kernel, n_mla=n_mla, w_ret=w_ret, w_m=w_m, prompt_tiles=prompt_tiles)

    def pair(width):
        return [pl.BlockSpec((tm, width), lambda i: (jnp.minimum(i, prompt_tiles - 1), 0)),
                pl.BlockSpec((tm, width), lambda i: (jnp.maximum(i - prompt_tiles, 0), 0))]

    return pl.pallas_call(
        kern,
        out_shape=jax.ShapeDtypeStruct((M, D), F32),
        grid=(M // tm,),
        in_specs=[pl.BlockSpec((tm, D), lambda i: (i, 0))] + pair(w_ret) + pair(w_m) + pair(n_mla * lat) + [
            pl.BlockSpec((None,) + w_uv.shape[1:], lambda i: (layer, 0, 0, 0)),
            pl.BlockSpec((None,) + w_out.shape[1:], lambda i: (layer, 0, 0)),
        ],
        out_specs=pl.BlockSpec((tm, D), lambda i: (i, 0)),
        scratch_shapes=[pltpu.VMEM((tm, n_mla * HEAD_DIM), BF16)],
        compiler_params=_params(("parallel",), 48),
        name="outproj",
    )(x, *y_ret, *y_m, *o_lat, w_uv, w_out)


def _final_norm_kernel(x_ref, g_ref, op_ref, os_ref, *, prompt_tiles):
    y = _rms(x_ref[...], g_ref[...])
    i = pl.program_id(0)

    @pl.when(i < prompt_tiles)
    def _():
        op_ref[...] = y

    @pl.when(i >= prompt_tiles)
    def _():
        os_ref[...] = y


def final_norm(x, g, n_p):
    M, D = x.shape
    n_s = M - n_p
    tm = _tile(math.gcd(n_p, n_s), 544, 8)
    prompt_tiles = n_p // tm
    return pl.pallas_call(
        functools.partial(_final_norm_kernel, prompt_tiles=prompt_tiles),
        out_shape=(jax.ShapeDtypeStruct((n_p, D), F32), jax.ShapeDtypeStruct((n_s, D), F32)),
        grid=(M // tm,),
        in_specs=[pl.BlockSpec((tm, D), lambda i: (i, 0)), pl.BlockSpec((1, D), lambda i: (0, 0))],
        out_specs=(pl.BlockSpec((tm, D), lambda i: (jnp.minimum(i, prompt_tiles - 1), 0)),
                   pl.BlockSpec((tm, D), lambda i: (jnp.maximum(i - prompt_tiles, 0), 0))),
        compiler_params=_params(("arbitrary",), 32),
        name="final_norm",
    )(x, g)


def _rope128(x, cos2, sin2):
    return x * cos2 + pltpu.roll(x, HEAD_DIM // 2, 1) * sin2


def _ret_block(q_ref, k_ref, v_ref, g_ref, cos_ref, sin_ref, dm_ref, qd_ref, kd_ref, gn_ref, y_ref,
               load_state, store_state, rowseq, n_heads, n_sub, cdec):
    cos2, sin2 = cos_ref[...], sin_ref[...]
    for h in range(n_heads):
        sl = slice(h * HEAD_DIM, (h + 1) * HEAD_DIM)
        qr = _rope128(q_ref[:, sl], cos2, sin2)
        kr = _rope128(k_ref[:, sl], cos2, sin2) * QK_SCALE
        vb = v_ref[:, sl].astype(BF16)
        a = lax.dot_general(qr.astype(BF16), kr.astype(BF16), NT_DIMS, preferred_element_type=F32) * dm_ref[h]
        o = jnp.dot(a.astype(BF16), vb, preferred_element_type=F32)
        qs = (qr * qd_ref[:, sl]).astype(BF16)
        ks = kr * kd_ref[:, sl]
        for i in range(n_sub):
            s_old = load_state(i, h)
            inter = jnp.dot(qs, s_old.astype(BF16), preferred_element_type=F32)
            if n_sub == 1:
                o = o + inter
                ks_i = ks
            else:
                sel = rowseq == float(i)
                o = o + jnp.where(sel, inter, 0.0)
                ks_i = jnp.where(sel, ks, 0.0)
            kv = lax.dot_general(ks_i.astype(BF16), vb, TN_DIMS, preferred_element_type=F32)
            store_state(i, h, s_old * cdec[h] + kv)
        yn = _head_ln(o, gn_ref[:, sl])
        y_ref[:, sl] = (jax.nn.silu(g_ref[:, sl]) * yn).astype(BF16)


def _ret_prompt_kernel(q_ref, k_ref, v_ref, g_ref, cos_ref, sin_ref, dm_ref, qd_ref, kd_ref, gn_ref,
                       y_ref, so_ref, s_ref, *, n_heads, cdec):
    c = pl.program_id(1)

    @pl.when(c == 0)
    def _():
        s_ref[...] = jnp.zeros_like(s_ref)

    def load_state(i, h):
        return s_ref[h]

    def store_state(i, h, val):
        s_ref[h] = val

    _ret_block(q_ref, k_ref, v_ref, g_ref, cos_ref, sin_ref, dm_ref, qd_ref, kd_ref, gn_ref, y_ref,
               load_state, store_state, None, n_heads, 1, cdec)

    @pl.when(c == pl.num_programs(1) - 1)
    def _():
        so_ref[...] = s_ref[...]


def _ret_sample_kernel(q_ref, k_ref, v_ref, g_ref, cos_ref, sin_ref, dm_ref, qd_ref, kd_ref, gn_ref,
                       rs_ref, s0_ref, y_ref, so_ref, *, n_heads, n_sub, cdec):
    def load_state(i, h):
        return s0_ref[i, h]

    def store_state(i, h, val):
        so_ref[i, h] = val

    _ret_block(q_ref, k_ref, v_ref, g_ref, cos_ref, sin_ref, dm_ref, qd_ref, kd_ref, gn_ref, y_ref,
               load_state, store_state, rs_ref[...], n_heads, n_sub, cdec)


def _ret_decay_tables(n_heads, chunk, n_sub):
    lg = np.log1p(-np.exp2(-5.0 - np.arange(n_heads, dtype=np.float64)))
    idx = np.arange(chunk, dtype=np.float64)
    diff = idx[:, None] - idx[None, :]
    dm1 = np.where(diff[None] >= 0, np.exp(np.maximum(diff, 0.0)[None] * lg[:, None, None]), 0.0)
    rows = n_sub * chunk
    dm = np.zeros((n_heads, rows, rows))
    for i in range(n_sub):
        dm[:, i * chunk:(i + 1) * chunk, i * chunk:(i + 1) * chunk] = dm1
    q_dec = np.exp((idx[:, None] + 1.0) * lg[None, :])
    k_dec = np.exp((chunk - 1.0 - idx)[:, None] * lg[None, :])
    qd = np.tile(np.repeat(q_dec, HEAD_DIM, axis=1), (n_sub, 1))
    kd = np.tile(np.repeat(k_dec, HEAD_DIM, axis=1), (n_sub, 1))
    cdec = tuple(float(np.float32(v)) for v in np.exp(chunk * lg))
    return jnp.asarray(dm, F32), jnp.asarray(qd, F32), jnp.asarray(kd, F32), cdec


def retention_prompt(proj, cos2, sin2, gn, B, L, n_heads):
    W = n_heads * HEAD_DIM
    C = min(CHUNK, L)
    nc = L // C
    dm, qd, kd, cdec = _ret_decay_tables(n_heads, C, 1)
    kern = functools.partial(_ret_prompt_kernel, n_heads=n_heads, cdec=cdec)
    col = lambda j: pl.BlockSpec((C, W), lambda b, c, j=j: (b * nc + c, j))
    const2 = lambda shape: pl.BlockSpec(shape, lambda b, c: (0,) * len(shape))
    return _Part(
        kern,
        args=(proj, proj, proj, proj, cos2, sin2, dm, qd, kd, gn),
        in_specs=[col(0), col(1), col(2), col(3),
                  pl.BlockSpec((C, HEAD_DIM), lambda b, c: (b * nc + c, 0)),
                  pl.BlockSpec((C, HEAD_DIM), lambda b, c: (b * nc + c, 0)),
                  const2(dm.shape), const2(qd.shape), const2(kd.shape), const2((1, W))],
        out_shape=[jax.ShapeDtypeStruct((B * L, W), BF16),
                   jax.ShapeDtypeStruct((B, n_heads, HEAD_DIM, HEAD_DIM), F32)],
        out_specs=[pl.BlockSpec((C, W), lambda b, c: (b * nc + c, 0)),
                   pl.BlockSpec((None, n_heads, HEAD_DIM, HEAD_DIM), lambda b, c: (b, 0, 0, 0))],
        scratch=[pltpu.VMEM((n_heads, HEAD_DIM, HEAD_DIM), F32)])


def retention_sample(proj, cos2, sin2, gn, state, layer, row0, DB, T, n_heads, G):
    W = n_heads * HEAD_DIM
    R = G * T
    rb0 = row0 // R
    dm, qd, kd, cdec = _ret_decay_tables(n_heads, T, G)
    rowseq = jnp.asarray(np.repeat(np.arange(G, dtype=np.float32), T)[:, None] * np.ones((1, HEAD_DIM), np.float32))
    kern = functools.partial(_ret_sample_kernel, n_heads=n_heads, n_sub=G, cdec=cdec)
    col = lambda j: pl.BlockSpec((R, W), lambda g, j=j: (rb0 + g, j))
    const1 = lambda shape: pl.BlockSpec(shape, lambda g: (0,) * len(shape))
    return _Part(
        kern,
        args=(proj, proj, proj, proj, cos2, sin2, dm, qd, kd, gn, rowseq, state),
        in_specs=[col(0), col(1), col(2), col(3),
                  pl.BlockSpec((R, HEAD_DIM), lambda g: (rb0 + g, 0)),
                  pl.BlockSpec((R, HEAD_DIM), lambda g: (rb0 + g, 0)),
                  const1(dm.shape), const1(qd.shape), const1(kd.shape), const1((1, W)),
                  const1(rowseq.shape),
                  pl.BlockSpec((None, G, n_heads, HEAD_DIM, HEAD_DIM), lambda g: (layer, g, 0, 0, 0))],
        out_shape=[jax.ShapeDtypeStruct((DB * T, W), BF16),
                   jax.ShapeDtypeStruct((DB, n_heads, HEAD_DIM, HEAD_DIM), F32)],
        out_specs=[pl.BlockSpec((R, W), lambda g: (g, 0)),
                   pl.BlockSpec((G, n_heads, HEAD_DIM, HEAD_DIM), lambda g: (g, 0, 0, 0))],
        scratch=[])


def _mlstm_block(u, v_ref, o_ref, gates, mprev_ref, mout_ref, nrows, tri, eye, lastm, gn_ref, y_ref,
                 load_c, store_c, store_n, rowseq, n_heads, n_sub, seq_len):
    W = n_heads * HEAD_DIM
    for h in range(n_heads):
        sl = slice(h * HEAD_DIM, (h + 1) * HEAD_DIM)
        q = u[:, h * HEAD_DIM:(h + 1) * HEAD_DIM]
        k = u[:, W + h * HEAD_DIM:W + (h + 1) * HEAD_DIM] * QK_SCALE
        qb, kb = q.astype(BF16), k.astype(BF16)
        vb = v_ref[:, sl].astype(BF16)
        ig_col = gates[:, h:h + 1]
        lf_col = gates[:, n_heads + h:n_heads + h + 1]
        m_col = mprev_ref[:, h:h + 1]
        f_row = jnp.sum(jnp.where(eye, lf_col, 0.0), axis=0, keepdims=True)
        bt_col = jnp.sum(jnp.where(tri, f_row, 0.0), axis=1, keepdims=True)
        bt_row = jnp.sum(jnp.where(eye, bt_col, 0.0), axis=0, keepdims=True)
        it_row = jnp.sum(jnp.where(eye, ig_col, 0.0), axis=0, keepdims=True)
        logw = jnp.where(tri, bt_col - bt_row + it_row, NEG_INF)
        log_inter = bt_col + m_col
        m_i = jnp.maximum(log_inter, jnp.max(logw, axis=1, keepdims=True))
        w = jnp.exp(logw - m_i)
        a = jnp.exp(log_inter - m_i)
        s = lax.dot_general(qb, kb, NT_DIMS, preferred_element_type=F32) * w
        num = jnp.dot(s.astype(BF16), vb, preferred_element_type=F32)
        den = jnp.sum(s, axis=1, keepdims=True) + a * jnp.sum(q * nrows(h), axis=1, keepdims=True)
        if n_sub == 1:
            rows = m_i.shape[0]
            m_last = jnp.broadcast_to(m_i[rows - 1:rows, :], m_i.shape)
            bt_last = jnp.broadcast_to(bt_col[rows - 1:rows, :], m_i.shape)
        else:
            m_row = jnp.sum(jnp.where(eye, m_i, 0.0), axis=0, keepdims=True)
            m_last = jnp.sum(jnp.where(lastm, m_row, 0.0), axis=1, keepdims=True)
            bt_last = jnp.sum(jnp.where(lastm, bt_row, 0.0), axis=1, keepdims=True)
        wl = jnp.exp(bt_last - bt_col + ig_col - m_last)
        al = jnp.exp(bt_last + m_col - m_last)
        kw = k * wl
        inter_sum = None
        for i in range(n_sub):
            c_old = load_c(i, h)
            inter = jnp.dot(qb, c_old.astype(BF16), preferred_element_type=F32)
            if n_sub == 1:
                kw_i = kw
            else:
                sel = rowseq == float(i)
                inter = jnp.where(sel, inter, 0.0)
                kw_i = jnp.where(sel, kw, 0.0)
            inter_sum = inter if inter_sum is None else inter_sum + inter
            al_i = al[i * seq_len:i * seq_len + 1, :]
            kv = lax.dot_general(kw_i.astype(BF16), vb, TN_DIMS, preferred_element_type=F32)
            store_c(i, h, al_i * c_old + kv)
            store_n(i, h, al_i, jnp.sum(kw_i, axis=0, keepdims=True))
        num = num + a * inter_sum
        hh = num / jnp.maximum(jnp.abs(den), jnp.exp(-m_i))
        yn = _head_ln(hh, gn_ref[:, sl])
        y_ref[:, sl] = (jax.nn.sigmoid(o_ref[:, sl]) * yn).astype(BF16)
        mout_ref[:, h:h + 1] = m_last


def _gate_values(g_ref, gb_ref, n_heads):
    pre = g_ref[...] + gb_ref[...]
    lane = lax.broadcasted_iota(jnp.int32, pre.shape, 1)
    lf = jnp.minimum(pre, 0.0) - jnp.log1p(jnp.exp(-jnp.abs(pre)))
    return jnp.where(lane < n_heads, pre, lf)


def _conv_silu(taps, cw_ref, cb_ref):
    acc = cb_ref[...]
    for j in range(CONV_W):
        acc = acc + taps[j] * cw_ref[j:j + 1, :]
    return jax.nn.silu(acc)


def _mlstm_prompt_kernel(x_ref, v_ref, o_ref, g_ref, cw_ref, cb_ref, gb_ref, gn_ref, tri_ref, eye_ref,
                         last_ref, y_ref, co_ref, no_ref, mo_ref,
                         xbuf_ref, c_ref, n_ref, m_ref, *, n_heads):
    c = pl.program_id(1)
    C = x_ref.shape[0]

    @pl.when(c == 0)
    def _():
        xbuf_ref[0:8, :] = jnp.zeros((8, xbuf_ref.shape[1]), F32)
        c_ref[...] = jnp.zeros_like(c_ref)
        n_ref[...] = jnp.zeros_like(n_ref)
        m_ref[...] = jnp.zeros_like(m_ref)

    xbuf_ref[8:8 + C, :] = x_ref[...]
    taps = [xbuf_ref[8 - (CONV_W - 1) + j:8 - (CONV_W - 1) + j + C, :] for j in range(CONV_W)]
    u = _conv_silu(taps, cw_ref, cb_ref)
    xbuf_ref[0:8, :] = x_ref[C - 8:C, :]
    gates = _gate_values(g_ref, gb_ref, n_heads)
    tri, eye, lastm = tri_ref[...] > 0.5, eye_ref[...] > 0.5, last_ref[...] > 0.5

    def load_c(i, h):
        return c_ref[h]

    def store_c(i, h, val):
        c_ref[h] = val

    def store_n(i, h, al_i, ksum):
        n_ref[h] = al_i * n_ref[h] + ksum

    def nrows(h):
        return n_ref[h]

    _mlstm_block(u, v_ref, o_ref, gates, m_ref, m_ref, nrows, tri, eye, lastm, gn_ref, y_ref,
                 load_c, store_c, store_n, None, n_heads, 1, C)

    @pl.when(c == pl.num_programs(1) - 1)
    def _():
        co_ref[...] = c_ref[...]
        no_ref[...] = n_ref[...]
        mo_ref[...] = m_ref[0:8, :]


def _mlstm_sample_kernel(x_ref, v_ref, o_ref, g_ref, cw_ref, cb_ref, gb_ref, gn_ref, tri_ref, eye_ref,
                         last_ref, rs_ref, rt_ref, st_ref, c0_ref, n0_ref, nr_ref, m0_ref,
                         y_ref, co_ref, no_ref, mo_ref, *, n_heads, n_sub, seq_len):
    R = x_ref.shape[0]
    x = x_ref[...]
    st = st_ref[...]
    tpos = rt_ref[:, 0:1]
    taps = []
    for j in range(CONV_W):
        s = CONV_W - 1 - j
        if s == 0:
            taps.append(x)
        else:
            cur = pltpu.roll(x, s, 0)
            old = pltpu.roll(st, (s - seq_len) % R, 0)
            taps.append(jnp.where(tpos >= float(s), cur, old))
    u = _conv_silu(taps, cw_ref, cb_ref)
    gates = _gate_values(g_ref, gb_ref, n_heads)
    tri, eye, lastm = tri_ref[...] > 0.5, eye_ref[...] > 0.5, last_ref[...] > 0.5
    mo_ref[...] = jnp.zeros_like(mo_ref)

    def load_c(i, h):
        return c0_ref[i, h]

    def store_c(i, h, val):
        co_ref[i, h] = val

    def store_n(i, h, al_i, ksum):
        sl = slice(h * HEAD_DIM, (h + 1) * HEAD_DIM)
        no_ref[i:i + 1, sl] = al_i * n0_ref[i:i + 1, sl] + ksum

    def nrows(h):
        return nr_ref[:, h * HEAD_DIM:(h + 1) * HEAD_DIM]

    _mlstm_block(u, v_ref, o_ref, gates, m0_ref, mo_ref, nrows, tri, eye, lastm, gn_ref, y_ref,
                 load_c, store_c, store_n, rs_ref[...], n_heads, n_sub, seq_len)


def _seq_masks(n_sub, seq_len):
    rows = n_sub * seq_len
    r = np.arange(rows)
    seq, t = r // seq_len, r % seq_len
    same = seq[:, None] == seq[None, :]
    tri = (same & (t[None, :] <= t[:, None])).astype(np.float32)
    eye = np.eye(rows, dtype=np.float32)
    last = (r[None, :] == (seq[:, None] * seq_len + seq_len - 1)).astype(np.float32)
    return jnp.asarray(tri), jnp.asarray(eye), jnp.asarray(last)


def mlstm_prompt(proj, conv_w, conv_b, gate_b, gn, B, L, n_heads, gate_blk):
    W = n_heads * HEAD_DIM
    C = min(CHUNK, L)
    nc = L // C
    tri, eye, last = _seq_masks(1, C)
    kern = functools.partial(_mlstm_prompt_kernel, n_heads=n_heads)
    const2 = lambda shape: pl.BlockSpec(shape, lambda b, c: (0,) * len(shape))
    return _Part(
        kern,
        args=(proj, proj, proj, proj, conv_w, conv_b, gate_b, gn, tri, eye, last),
        in_specs=[pl.BlockSpec((C, 2 * W), lambda b, c: (b * nc + c, 2)),
                  pl.BlockSpec((C, W), lambda b, c: (b * nc + c, 6)),
                  pl.BlockSpec((C, W), lambda b, c: (b * nc + c, 7)),
                  pl.BlockSpec((C, HEAD_DIM), lambda b, c: (b * nc + c, gate_blk)),
                  const2((CONV_W, 2 * W)), const2((1, 2 * W)), const2((1, HEAD_DIM)), const2((1, W)),
                  const2(tri.shape), const2(eye.shape), const2(last.shape)],
        out_shape=[jax.ShapeDtypeStruct((B * L, W), BF16),
                   jax.ShapeDtypeStruct((B, n_heads, HEAD_DIM, HEAD_DIM), F32),
                   jax.ShapeDtypeStruct((B, n_heads, 1, HEAD_DIM), F32),
                   jax.ShapeDtypeStruct((B, 8, HEAD_DIM), F32)],
        out_specs=[pl.BlockSpec((C, W), lambda b, c: (b * nc + c, 0)),
                   pl.BlockSpec((None, n_heads, HEAD_DIM, HEAD_DIM), lambda b, c: (b, 0, 0, 0)),
                   pl.BlockSpec((None, n_heads, 1, HEAD_DIM), lambda b, c: (b, 0, 0, 0)),
                   pl.BlockSpec((None, 8, HEAD_DIM), lambda b, c: (b, 0, 0))],
        scratch=[pltpu.VMEM((8 + C, 2 * W), F32),
                 pltpu.VMEM((n_heads, HEAD_DIM, HEAD_DIM), F32),
                 pltpu.VMEM((n_heads, 1, HEAD_DIM), F32),
                 pltpu.VMEM((C, HEAD_DIM), F32)])


def mlstm_sample(proj, conv_w, conv_b, gate_b, gn, conv_rows, c_state, n_state, n_rows, m_rows,
                 layer, row0, DB, T, n_heads, gate_blk, G):
    W = n_heads * HEAD_DIM
    R = G * T
    rb0 = row0 // R
    tri, eye, last = _seq_masks(G, T)
    ones = np.ones((1, HEAD_DIM), np.float32)
    rowseq = jnp.asarray(np.repeat(np.arange(G, dtype=np.float32), T)[:, None] * ones)
    rowt = jnp.asarray(np.tile(np.arange(T, dtype=np.float32), G)[:, None] * ones)
    kern = functools.partial(_mlstm_sample_kernel, n_heads=n_heads, n_sub=G, seq_len=T)
    const1 = lambda shape: pl.BlockSpec(shape, lambda g: (0,) * len(shape))
    return _Part(
        kern,
        args=(proj, proj, proj, proj, conv_w, conv_b, gate_b, gn, tri, eye, last, rowseq, rowt,
              conv_rows, c_state, n_state, n_rows, m_rows),
        out_shape=[jax.ShapeDtypeStruct((DB * T, W), BF16),
                   jax.ShapeDtypeStruct((DB, n_heads, HEAD_DIM, HEAD_DIM), F32),
                   jax.ShapeDtypeStruct((DB, W), F32),
                   jax.ShapeDtypeStruct((DB * T, HEAD_DIM), F32)],
        in_specs=[pl.BlockSpec((R, 2 * W), lambda g: (rb0 + g, 2)),
                  pl.BlockSpec((R, W), lambda g: (rb0 + g, 6)),
                  pl.BlockSpec((R, W), lambda g: (rb0 + g, 7)),
                  pl.BlockSpec((R, HEAD_DIM), lambda g: (rb0 + g, gate_blk)),
                  const1((CONV_W, 2 * W)), const1((1, 2 * W)), const1((1, HEAD_DIM)), const1((1, W)),
                  const1(tri.shape), const1(eye.shape), const1(last.shape),
                  const1(rowseq.shape), const1(rowt.shape),
                  pl.BlockSpec((None, R, 2 * W), lambda g: (layer, g, 0)),
                  pl.BlockSpec((None, G, n_heads, HEAD_DIM, HEAD_DIM), lambda g: (layer, g, 0, 0, 0)),
                  pl.BlockSpec((None, G, W), lambda g: (layer, g, 0)),
                  pl.BlockSpec((None, R, W), lambda g: (layer, g, 0)),
                  pl.BlockSpec((None, R, HEAD_DIM), lambda g: (layer, g, 0))],
        out_specs=[pl.BlockSpec((R, W), lambda g: (g, 0)),
                   pl.BlockSpec((G, n_heads, HEAD_DIM, HEAD_DIM), lambda g: (g, 0, 0, 0)),
                   pl.BlockSpec((G, W), lambda g: (g, 0)),
                   pl.BlockSpec((R, HEAD_DIM), lambda g: (g, 0))],
        scratch=[])


def _rope64x2(x, cos4, sin_lo, sin_hi):
    return x * cos4 + pltpu.roll(x, 96, 1) * sin_lo + pltpu.roll(x, 32, 1) * sin_hi


def _mla_prep_kernel(qn_ref, qp_ref, kv_ref, cos_ref, slo_ref, shi_ref, kvn_ref, wuk_ref,
                     kvr_ref, kb_ref, q_ref, *, n_heads, lat):
    cos4, slo, shi = cos_ref[...], slo_ref[...], shi_ref[...]
    c_new = _rms(kv_ref[:, 0:lat], kvn_ref[...])
    kr = _rope64x2(kv_ref[:, lat:lat + HEAD_DIM], cos4, slo, shi)
    kvr_ref[:, 0:lat] = c_new
    kvr_ref[:, lat:lat + HEAD_DIM] = kr
    kb_ref[:, 0:lat] = c_new.astype(BF16)
    kb_ref[:, lat:lat + HEAD_DIM] = (kr + pltpu.roll(kr, HEAD_DIM // 2, 1)).astype(BF16)
    qw = lat + HEAD_DIM
    lane = lax.broadcasted_iota(jnp.int32, (qp_ref.shape[0], HEAD_DIM), 1)
    for p in range(n_heads // 2):
        pe = _rope64x2(qp_ref[:, p * HEAD_DIM:(p + 1) * HEAD_DIM], cos4, slo, shi)
        for half in range(2):
            h = 2 * p + half
            q_lat = jnp.dot(qn_ref[:, h * HEAD_DIM:(h + 1) * HEAD_DIM].astype(BF16), wuk_ref[h],
                            preferred_element_type=F32)
            q_ref[:, h * qw:h * qw + lat] = q_lat.astype(BF16)
            keep = (lane < HEAD_DIM // 2) if half == 0 else (lane >= HEAD_DIM // 2)
            q_ref[:, h * qw + lat:(h + 1) * qw] = jnp.where(keep, pe, 0.0).astype(BF16)


def mla_prep(proj, cos4, sin_lo, sin_hi, kv_norm, w_uk_t, n_heads, lat):
    M = proj.shape[0]
    tm = _tile(M, 272, 16)
    qw = lat + HEAD_DIM
    kern = functools.partial(_mla_prep_kernel, n_heads=n_heads, lat=lat)
    row = lambda w, j: pl.BlockSpec((tm, w), lambda i, j=j: (i, j))
    const1 = lambda shape: pl.BlockSpec(shape, lambda i: (0,) * len(shape))
    nope_w, pe_w = n_heads * HEAD_DIM, n_heads * HEAD_DIM // 2
    return pl.pallas_call(
        kern,
        out_shape=(jax.ShapeDtypeStruct((M, qw), F32),
                   jax.ShapeDtypeStruct((M, qw), BF16),
                   jax.ShapeDtypeStruct((M, n_heads * qw), BF16)),
        grid=(M // tm,),
        in_specs=[row(nope_w, 4), row(pe_w, 10), row(qw, 15),
                  row(HEAD_DIM, 0), row(HEAD_DIM, 0), row(HEAD_DIM, 0),
                  const1((1, lat)), const1(w_uk_t.shape)],
        out_specs=(row(qw, 0), row(qw, 0), row(n_heads * qw, 0)),
        compiler_params=_params(("parallel",), 32),
        name="mla_prep",
    )(proj, proj, proj, cos4, sin_lo, sin_hi, kv_norm, w_uk_t)


def _mla_prompt_kernel(q_ref, kb_ref, o_ref, qt_ref, m_ref, l_ref, acc_ref, *, n_heads, lat, tq, scale):
    i = pl.program_id(1)
    qw = lat + HEAD_DIM
    cols = n_heads * tq
    for h in range(n_heads):
        qt_ref[:, h * tq:(h + 1) * tq] = q_ref[:, h * qw:(h + 1) * qw].astype(F32).T.astype(BF16)
    m_ref[...] = jnp.full(m_ref.shape, NEG_INF, F32)
    l_ref[...] = jnp.zeros_like(l_ref)
    acc_ref[...] = jnp.zeros_like(acc_ref)

    def chunk(j, masked):
        kblk = kb_ref[pl.ds(pl.multiple_of(j * tq, tq), tq), :]
        s = jnp.dot(kblk, qt_ref[...], preferred_element_type=F32) * scale
        if masked:
            kloc = lax.broadcasted_iota(jnp.int32, (tq, cols), 0)
            qloc = lax.broadcasted_iota(jnp.int32, (tq, cols), 1) & (tq - 1)
            s = jnp.where(kloc <= qloc, s, NEG_INF)
        m_old = m_ref[...]
        m_new = jnp.maximum(m_old, jnp.max(s, axis=0, keepdims=True))
        alpha = jnp.exp(m_old - m_new)
        p = jnp.exp(s - m_new)
        l_ref[...] = alpha * l_ref[...] + jnp.sum(p, axis=0, keepdims=True)
        m_ref[...] = m_new
        vt = kblk[:, 0:lat].astype(F32).T.astype(BF16)
        acc_ref[...] = alpha * acc_ref[...] + jnp.dot(vt, p.astype(BF16), preferred_element_type=F32)

    def body(j, carry):
        chunk(j, False)
        return carry

    lax.fori_loop(0, i, body, 0)
    chunk(i, True)
    ot = acc_ref[...] / l_ref[...]
    for h in range(n_heads):
        o_ref[:, h * lat:(h + 1) * lat] = ot[:, h * tq:(h + 1) * tq].T.astype(BF16)


def mla_prompt(q, kb, B, L, n_heads, lat, scale):
    qw = lat + HEAD_DIM
    tq = _tile(L, 256, 128)
    assert tq & (tq - 1) == 0
    nq = L // tq
    cols = n_heads * tq
    kern = functools.partial(_mla_prompt_kernel, n_heads=n_heads, lat=lat, tq=tq, scale=scale)
    return pl.pallas_call(
        kern,
        out_shape=jax.ShapeDtypeStruct((B * L, n_heads * lat), BF16),
        grid=(B, nq),
        in_specs=[pl.BlockSpec((tq, n_heads * qw), lambda b, i: (b * nq + i, 0)),
                  pl.BlockSpec((L, qw), lambda b, i: (b, 0))],
        out_specs=pl.BlockSpec((tq, n_heads * lat), lambda b, i: (b * nq + i, 0)),
        scratch_shapes=[pltpu.VMEM((qw, cols), BF16),
                        pltpu.VMEM((1, cols), F32),
                        pltpu.VMEM((1, cols), F32),
                        pltpu.VMEM((lat, cols), F32)],
        compiler_params=_params(("parallel", "arbitrary"), 48),
        name="mla_prompt",
    )(q, kb)


def _mla_sample_kernel(pt_ref, q_ref, kn_ref, kv_hbm, krt_hbm, o_ref, kvbuf, krbuf, cbuf_ref, rbuf_ref, sem,
                       *, layer, n_pages, lat, n_heads, scale):
    b = pl.program_id(0)
    nb = pl.num_programs(0)
    slot = b & 1
    page = kvbuf.shape[2]
    rope_w = krbuf.shape[2]

    def page_copies(seq, dst_slot):
        copies = []
        for p in range(n_pages):
            pg = pt_ref[seq * n_pages + p]
            copies.append(pltpu.make_async_copy(kv_hbm.at[layer, pg], kvbuf.at[dst_slot, p], sem.at[0, dst_slot]))
            copies.append(pltpu.make_async_copy(krt_hbm.at[layer, pg], krbuf.at[dst_slot, p], sem.at[1, dst_slot]))
        return copies

    @pl.when(b == 0)
    def _():
        for cp in page_copies(0, 0):
            cp.start()

    for cp in page_copies(b, slot):
        cp.wait()
    nxt = jnp.minimum(b + 1, nb - 1)
    for cp in page_copies(nxt, 1 - slot):
        cp.start()

    for p in range(n_pages):
        cbuf_ref[p * page:(p + 1) * page, :] = kvbuf[slot, p].astype(BF16)
        rbuf_ref[:, p * page:(p + 1) * page] = krbuf[slot, p].astype(BF16)
    q = q_ref[...]
    cb = cbuf_ref[...]
    q_pe = q[:, lat:lat + rope_w] + q[:, lat + rope_w:]
    s = (lax.dot_general(q[:, 0:lat], cb, NT_DIMS, preferred_element_type=F32)
         + jnp.dot(q_pe, rbuf_ref[...], preferred_element_type=F32)) * scale
    kn = kn_ref[...]
    sn = lax.dot_general(q, kn, NT_DIMS, preferred_element_type=F32) * scale
    t_q = jnp.right_shift(lax.broadcasted_iota(jnp.int32, sn.shape, 0), n_heads.bit_length() - 1)
    t_k = lax.broadcasted_iota(jnp.int32, sn.shape, 1)
    sn = jnp.where(t_k <= t_q, sn, NEG_INF)
    m = jnp.maximum(jnp.max(s, axis=1, keepdims=True), jnp.max(sn, axis=1, keepdims=True))
    pc = jnp.exp(s - m)
    pn = jnp.exp(sn - m)
    denom = jnp.sum(pc, axis=1, keepdims=True) + jnp.sum(pn, axis=1, keepdims=True)
    acc = (jnp.dot(pc.astype(BF16), cb, preferred_element_type=F32)
           + jnp.dot(pn.astype(BF16), kn[:, 0:lat], preferred_element_type=F32))
    o_ref[...] = (acc / denom).astype(BF16)

    @pl.when(b == nb - 1)
    def _():
        for cp in page_copies(nxt, 1 - slot):
            cp.wait()


def mla_sample(q_s, kb_new, cache_kv, cache_krt, page_table, layer, n_heads, lat, scale):
    DB, rows, qw = q_s.shape
    n_pages = page_table.shape[1]
    page = cache_kv.shape[2]
    rope_w = cache_krt.shape[2]
    seq_len = rows // n_heads
    assert seq_len <= page and n_heads & (n_heads - 1) == 0 and 2 * rope_w == HEAD_DIM
    assert n_pages <= MAX_PAGES_IN_VMEM, "a sequence's cache pages are held in VMEM twice (two slots)"
    kern = functools.partial(_mla_sample_kernel, layer=layer, n_pages=n_pages, lat=lat, n_heads=n_heads,
                             scale=scale)
    grid_spec = pltpu.PrefetchScalarGridSpec(
        num_scalar_prefetch=1,
        grid=(DB,),
        in_specs=[pl.BlockSpec((None, rows, qw), lambda b, pt: (b, 0, 0)),
                  pl.BlockSpec((None, page, qw), lambda b, pt: (b, 0, 0)),
                  pl.BlockSpec(memory_space=pl.ANY),
                  pl.BlockSpec(memory_space=pl.ANY)],
        out_specs=pl.BlockSpec((None, rows, lat), lambda b, pt: (b, 0, 0)),
        scratch_shapes=[pltpu.VMEM((2, n_pages, page, lat), F32),
                        pltpu.VMEM((2, n_pages, rope_w, page), F32),
                        pltpu.VMEM((n_pages * page, lat), BF16),
                        pltpu.VMEM((rope_w, n_pages * page), BF16),
                        pltpu.SemaphoreType.DMA((2, 2))])
    return pl.pallas_call(
        kern,
        out_shape=jax.ShapeDtypeStruct((DB, rows, lat), BF16),
        grid_spec=grid_spec,
        compiler_params=_params(("arbitrary",), 56),
        name="mla_sample",
    )(page_table.reshape(-1), q_s, kb_new, cache_kv, cache_krt)


def _rope_tables(pos, dim):
    inv = ROPE_BASE ** (-jnp.arange(0, dim, 2, dtype=F32) / dim)
    ang = pos.astype(F32)[:, None] * inv[None, :]
    return jnp.cos(ang), jnp.sin(ang)


def _relayout_w_in(w_in, n_mla, nope, rope_w, lat):
    depth, D, _ = w_in.shape
    main = 4 * (D // 4) + D
    n_gate = 2 * (D // (4 * HEAD_DIM))
    w = w_in.astype(BF16)
    gates = w[..., main:main + n_gate]
    q0 = main + n_gate
    mlq = w[..., q0:q0 + n_mla * (nope + rope_w)].reshape(depth, D, n_mla, nope + rope_w)
    q_nope = mlq[..., :nope].reshape(depth, D, n_mla * nope)
    q_pe = mlq[..., nope:].reshape(depth, D, n_mla * rope_w)
    k0 = q0 + n_mla * (nope + rope_w)
    mkv = w[..., k0:k0 + lat + rope_w]
    zeros = lambda n: jnp.zeros((depth, D, n), BF16)
    return jnp.concatenate([w[..., :main], q_nope, q_pe, gates, zeros(HEAD_DIM - n_gate),
                            mkv, zeros(HEAD_DIM - rope_w)], axis=-1)


def kernel(x_prompt, x_sample, cache_mla_kv, cache_mla_kr, page_table, state_ret, state_mlstm_C, state_mlstm_n, state_mlstm_m, state_mlstm_conv, norm_attn, norm_mlp, norm_final, w_in, ret_gn, mlstm_conv_w, mlstm_conv_b, mlstm_gate_b, mlstm_gn, mla_kv_norm, w_uk, w_uv, w_out, w_up, w_down):
    B, L, D = x_prompt.shape
    DB, T, _ = x_sample.shape
    depth = w_in.shape[0]
    n_heads = D // (4 * HEAD_DIM)
    W = n_heads * HEAD_DIM
    n_mla = D // (2 * HEAD_DIM)
    lat = mla_kv_norm.shape[1]
    rope_w = cache_mla_kr.shape[3]
    page = cache_mla_kv.shape[2]
    past_len = page_table.shape[1] * page
    assert D == 2048 and lat == 2 * HEAD_DIM and rope_w == HEAD_DIM // 2 and T >= CONV_W - 1 and L % min(CHUNK, L) == 0
    mla_scale = (HEAD_DIM + rope_w) ** -0.5
    G = 8
    n_p, n_s = B * L, DB * T
    gate_blk = (4 * W + 4 * W + n_mla * HEAD_DIM + n_mla * rope_w) // HEAD_DIM
    assert n_p % (G * T) == 0 and DB % G == 0

    pos = jnp.concatenate([jnp.tile(jnp.arange(L, dtype=jnp.int32), B),
                           jnp.tile(past_len + jnp.arange(T, dtype=jnp.int32), DB)])
    cos_h, sin_h = _rope_tables(pos, HEAD_DIM)
    cos2 = jnp.concatenate([cos_h, cos_h], axis=1)
    sin2 = jnp.concatenate([-sin_h, sin_h], axis=1)
    cos_r, sin_r = _rope_tables(pos, rope_w)
    zero_r = jnp.zeros_like(sin_r)
    cos4 = jnp.concatenate([cos_r] * 4, axis=1)
    sin_lo = jnp.concatenate([-sin_r, zero_r, -sin_r, zero_r], axis=1)
    sin_hi = jnp.concatenate([zero_r, sin_r, zero_r, sin_r], axis=1)

    w_in_b = _relayout_w_in(w_in, n_mla, HEAD_DIM, rope_w, lat)
    w_out_b, w_up_b, w_down_b = w_out.astype(BF16), w_up.astype(BF16), w_down.astype(BF16)
    w_uk_t = jnp.transpose(w_uk, (0, 2, 3, 1)).astype(BF16)
    w_uv_t = jnp.transpose(w_uv, (0, 2, 1, 3)).astype(BF16)
    gate_b = jnp.pad(mlstm_gate_b, ((0, 0), (0, HEAD_DIM - mlstm_gate_b.shape[1])))

    conv_rows = jnp.pad(state_mlstm_conv, ((0, 0), (0, 0), (T - (CONV_W - 1), 0), (0, 0))).reshape(depth, n_s, 2 * W)
    n_state = state_mlstm_n.reshape(depth, DB, W)
    n_rows = jnp.repeat(n_state, T, axis=1)
    m_rows = jnp.repeat(jnp.pad(state_mlstm_m, ((0, 0), (0, 0), (0, HEAD_DIM - n_heads))), T, axis=1)

    cache_krt = jnp.swapaxes(cache_mla_kr, 2, 3)

    h = jnp.concatenate([x_prompt.reshape(n_p, D), x_sample.reshape(n_s, D)], axis=0)
    outs = [[] for _ in range(14)]
    for l in range(depth):
        proj = norm_matmul(h, norm_attn[l][None], w_in_b, l)

        gn_r = ret_gn[l][None]
        cw, cb, gb, gn_m = mlstm_conv_w[l], mlstm_conv_b[l][None], gate_b[l][None], mlstm_gn[l][None]
        (yr_p, ret_p), (ym_p, c_p, nn_p, mm_p) = _run_parts(
            [retention_prompt(proj, cos2, sin2, gn_r, B, L, n_heads),
             mlstm_prompt(proj, cw, cb, gb, gn_m, B, L, n_heads, gate_blk)],
            (B, L // min(CHUNK, L)), ("parallel", "arbitrary"), 40, "recurrent_prompt")
        (yr_s, ret_s), (ym_s, c_s, nn_s, mm_s) = _run_parts(
            [retention_sample(proj, cos2, sin2, gn_r, state_ret, l, n_p, DB, T, n_heads, G),
             mlstm_sample(proj, cw, cb, gb, gn_m, conv_rows, state_mlstm_C, n_state,
                          n_rows, m_rows, l, n_p, DB, T, n_heads, gate_blk, G)],
            (DB // G,), ("parallel",), 40, "recurrent_sample")

        kvr, kb, q = mla_prep(proj, cos4, sin_lo, sin_hi, mla_kv_norm[l][None], w_uk_t[l], n_mla, lat)
        o_p = mla_prompt(q, kb, B, L, n_mla, lat, mla_scale)
        q_s = q[n_p:].reshape(DB, T * n_mla, lat + HEAD_DIM)
        kb_new = jnp.pad(kb[n_p:].reshape(DB, T, lat + HEAD_DIM), ((0, 0), (0, page - T), (0, 0)))
        o_s = mla_sample(q_s, kb_new, cache_mla_kv, cache_krt, page_table, l, n_mla, lat, mla_scale)

        h = outproj_residual(h, (yr_p, yr_s), (ym_p, ym_s), (o_p, o_s.reshape(n_s, n_mla * lat)),
                             w_uv_t, w_out_b, l)
        h = mlp_residual(h, norm_mlp[l][None], w_up_b, w_down_b, l)

        conv_p = jnp.stack([proj[b * L + L - (CONV_W - 1):(b + 1) * L, 4 * W:6 * W] for b in range(B)])
        conv_s = proj[n_p:, 4 * W:6 * W].reshape(DB, T, 2 * W)[:, T - (CONV_W - 1):]
        per_layer = (
            kvr[:n_p, :lat].reshape(B, L, lat), kvr[:n_p, lat:lat + rope_w].reshape(B, L, rope_w),
            kvr[n_p:, :lat].reshape(DB, T, lat), kvr[n_p:, lat:lat + rope_w].reshape(DB, T, rope_w),
            ret_p, ret_s, c_p, c_s,
            nn_p.reshape(B, n_heads, HEAD_DIM), nn_s.reshape(DB, n_heads, HEAD_DIM),
            mm_p[:, 0, :n_heads], mm_s.reshape(DB, T, HEAD_DIM)[:, T - 1, :n_heads],
            conv_p, conv_s,
        )
        for acc, val in zip(outs, per_layer):
            acc.append(val)

    y_p, y_s = final_norm(h, norm_final[None], n_p)
    return (y_p.reshape(B, L, D), y_s.reshape(DB, T, D)) + tuple(jnp.stack(o) for o in outs)
```

```python
import functools
import math

import numpy as np
import jax
import jax.numpy as jnp
from jax import lax
from jax.experimental import pallas as pl
from jax.experimental.pallas import tpu as pltpu

F32 = jnp.float32
BF16 = jnp.bfloat16

HEAD_DIM = 128
CONV_W = 4
CHUNK = 128
ROPE_BASE = 10000.0
EPS = 1e-6
QK_SCALE = HEAD_DIM ** -0.5
NEG_INF = float("-inf")

MAX_PAGES_IN_VMEM = 64

NT_DIMS = (((1,), (1,)), ((), ()))
TN_DIMS = (((0,), (0,)), ((), ()))


def _params(semantics, vmem_mb):
    return pltpu.CompilerParams(dimension_semantics=semantics, vmem_limit_bytes=vmem_mb << 20)


def _tile(n, target, mult):
    best = None
    for t in range(mult, min(n, target) + 1, mult):
        if n % t == 0:
            best = t
    assert best is not None, (n, target, mult)
    return best


class _Part:
    def __init__(self, body, args, in_specs, out_shape, out_specs, scratch):
        self.body, self.args, self.in_specs = body, list(args), list(in_specs)
        self.out_shape, self.out_specs, self.scratch = list(out_shape), list(out_specs), list(scratch)


def _run_parts(parts, grid, semantics, vmem_mb, name):
    n_in = [len(p.args) for p in parts]
    n_out = [len(p.out_shape) for p in parts]
    n_scr = [len(p.scratch) for p in parts]

    def body(*refs):
        ins, outs, scr = refs[:sum(n_in)], refs[sum(n_in):sum(n_in) + sum(n_out)], refs[sum(n_in) + sum(n_out):]
        i0 = o0 = s0 = 0
        for p, ni, no, ns in zip(parts, n_in, n_out, n_scr):
            p.body(*ins[i0:i0 + ni], *outs[o0:o0 + no], *scr[s0:s0 + ns])
            i0, o0, s0 = i0 + ni, o0 + no, s0 + ns

    res = pl.pallas_call(
        body,
        out_shape=tuple(s for p in parts for s in p.out_shape),
        grid=grid,
        in_specs=[s for p in parts for s in p.in_specs],
        out_specs=tuple(s for p in parts for s in p.out_specs),
        scratch_shapes=[s for p in parts for s in p.scratch],
        compiler_params=_params(semantics, vmem_mb),
        name=name,
    )(*[a for p in parts for a in p.args])
    split, o0 = [], 0
    for no in n_out:
        split.append(res[o0:o0 + no])
        o0 += no
    return split


def _rms(x, g):
    ms = jnp.mean(x * x, axis=-1, keepdims=True)
    return (x * lax.rsqrt(ms + EPS)) * g


def _head_ln(y, g):
    mu = jnp.mean(y, axis=-1, keepdims=True)
    yc = y - mu
    var = jnp.mean(yc * yc, axis=-1, keepdims=True)
    return (yc * lax.rsqrt(var + EPS)) * g


def _norm_matmul_kernel(x_ref, g_ref, w_ref, o_ref, xn_ref):
    @pl.when(pl.program_id(1) == 0)
    def _():
        xn_ref[...] = _rms(x_ref[...], g_ref[...]).astype(BF16)

    o_ref[...] = jnp.dot(xn_ref[...], w_ref[...], preferred_element_type=F32)


def norm_matmul(x, g, w, layer):
    M, D = x.shape
    N = w.shape[2]
    tm = _tile(M, 1088, 16)
    tn = _tile(N, 1024, 128)
    return pl.pallas_call(
        _norm_matmul_kernel,
        out_shape=jax.ShapeDtypeStruct((M, N), F32),
        grid=(M // tm, N // tn),
        in_specs=[
            pl.BlockSpec((tm, D), lambda i, j: (i, 0)),
            pl.BlockSpec((1, D), lambda i, j: (0, 0)),
            pl.BlockSpec((None, D, tn), lambda i, j: (layer, 0, j)),
        ],
        out_specs=pl.BlockSpec((tm, tn), lambda i, j: (i, j)),
        scratch_shapes=[pltpu.VMEM((tm, D), BF16)],
        compiler_params=_params(("parallel", "arbitrary"), 48),
        name="norm_inproj",
    )(x, g, w)


def _mlp_kernel(x_ref, g_ref, wu_ref, wd_ref, o_ref, xn_ref):
    @pl.when(pl.program_id(1) == 0)
    def _():
        x = x_ref[...]
        xn_ref[...] = _rms(x, g_ref[...]).astype(BF16)
        o_ref[...] = x

    u = jnp.dot(xn_ref[...], wu_ref[...], preferred_element_type=F32)
    a = jnp.maximum(u, 0.0)
    o_ref[...] += jnp.dot((a * a).astype(BF16), wd_ref[...], preferred_element_type=F32)


def mlp_residual(x, g, w_up, w_down, layer):
    M, D = x.shape
    F = w_up.shape[2]
    tm = _tile(M, 1088, 16)
    tf = _tile(F, 1024, 128)
    return pl.pallas_call(
        _mlp_kernel,
        out_shape=jax.ShapeDtypeStruct((M, D), F32),
        grid=(M // tm, F // tf),
        in_specs=[
            pl.BlockSpec((tm, D), lambda i, f: (i, 0)),
            pl.BlockSpec((1, D), lambda i, f: (0, 0)),
            pl.BlockSpec((None, D, tf), lambda i, f: (layer, 0, f)),
            pl.BlockSpec((None, tf, D), lambda i, f: (layer, f, 0)),
        ],
        out_specs=pl.BlockSpec((tm, D), lambda i, f: (i, 0), pipeline_mode=pl.Buffered(1)),
        scratch_shapes=[pltpu.VMEM((tm, D), BF16)],
        compiler_params=_params(("parallel", "arbitrary"), 56),
        name="mlp",
    )(x, g, w_up, w_down)


def _outproj_kernel(x_ref, yrp_ref, yrs_ref, ymp_ref, yms_ref, op_ref, os_ref, wuv_ref, w_ref, out_ref, ymla_ref,
                    *, n_mla, w_ret, w_m, prompt_tiles):
    lat = wuv_ref.shape[1]
    is_prompt = pl.program_id(0) < prompt_tiles
    o_lat = jnp.where(is_prompt, op_ref[...], os_ref[...])
    for h in range(n_mla):
        yh = jnp.dot(o_lat[:, h * lat:(h + 1) * lat], wuv_ref[h], preferred_element_type=F32)
        ymla_ref[:, h * HEAD_DIM:(h + 1) * HEAD_DIM] = yh.astype(BF16)
    acc = x_ref[...]
    acc += jnp.dot(jnp.where(is_prompt, yrp_ref[...], yrs_ref[...]), w_ref[0:w_ret, :], preferred_element_type=F32)
    acc += jnp.dot(jnp.where(is_prompt, ymp_ref[...], yms_ref[...]), w_ref[w_ret:w_ret + w_m, :],
                   preferred_element_type=F32)
    acc += jnp.dot(ymla_ref[...], w_ref[w_ret + w_m:, :], preferred_element_type=F32)
    out_ref[...] = acc


def outproj_residual(x, y_ret, y_m, o_lat, w_uv, w_out, layer):
    M, D = x.shape
    _, n_mla, lat, _ = w_uv.shape
    n_p, n_s = o_lat[0].shape[0], o_lat[1].shape[0]
    w_ret, w_m = y_ret[0].shape[1], y_m[0].shape[1]
    tm = _tile(math.gcd(n_p, n_s), 512, 16)
    prompt_tiles = n_p // tm
    kern = functools.partial(_outproj_kernel, n_mla=n_mla, w_ret=w_ret, w_m=w_m, prompt_tiles=prompt_tiles)

    def pair(width):
        return [pl.BlockSpec((tm, width), lambda i: (jnp.minimum(i, prompt_tiles - 1), 0)),
                pl.BlockSpec((tm, width), lambda i: (jnp.maximum(i - prompt_tiles, 0), 0))]

    return pl.pallas_call(
        kern,
        out_shape=jax.ShapeDtypeStruct((M, D), F32),
        grid=(M // tm,),
        in_specs=[pl.BlockSpec((tm, D), lambda i: (i, 0))] + pair(w_ret) + pair(w_m) + pair(n_mla * lat) + [
            pl.BlockSpec((None,) + w_uv.shape[1:], lambda i: (layer, 0, 0, 0), pipeline_mode=pl.Buffered(1)),
            pl.BlockSpec((None,) + w_out.shape[1:], lambda i: (layer, 0, 0), pipeline_mode=pl.Buffered(1)),
        ],
        out_specs=pl.BlockSpec((tm, D), lambda i: (i, 0)),
        scratch_shapes=[pltpu.VMEM((tm, n_mla * HEAD_DIM), BF16)],
        compiler_params=_params(("parallel",), 48),
        name="outproj",
    )(x, *y_ret, *y_m, *o_lat, w_uv, w_out)


def _final_norm_kernel(x_ref, g_ref, op_ref, os_ref, *, prompt_tiles):
    y = _rms(x_ref[...], g_ref[...])
    i = pl.program_id(0)

    @pl.when(i < prompt_tiles)
    def _():
        op_ref[...] = y

    @pl.when(i >= prompt_tiles)
    def _():
        os_ref[...] = y


def final_norm(x, g, n_p):
    M, D = x.shape
    n_s = M - n_p
    tm = _tile(math.gcd(n_p, n_s), 544, 8)
    prompt_tiles = n_p // tm
    return pl.pallas_call(
        functools.partial(_final_norm_kernel, prompt_tiles=prompt_tiles),
        out_shape=(jax.ShapeDtypeStruct((n_p, D), F32), jax.ShapeDtypeStruct((n_s, D), F32)),
        grid=(M // tm,),
        in_specs=[pl.BlockSpec((tm, D), lambda i: (i, 0)), pl.BlockSpec((1, D), lambda i: (0, 0))],
        out_specs=(pl.BlockSpec((tm, D), lambda i: (jnp.minimum(i, prompt_tiles - 1), 0)),
                   pl.BlockSpec((tm, D), lambda i: (jnp.maximum(i - prompt_tiles, 0), 0))),
        compiler_params=_params(("arbitrary",), 32),
        name="final_norm",
    )(x, g)


def _rope128(x, cos2, sin2):
    return x * cos2 + pltpu.roll(x, HEAD_DIM // 2, 1) * sin2


def _ret_block(q_ref, k_ref, v_ref, g_ref, cos_ref, sin_ref, dm_ref, qd_ref, kd_ref, gn_ref, y_ref,
               load_state, store_state, rowseq, n_heads, n_sub, cdec):
    cos2, sin2 = cos_ref[...], sin_ref[...]
    for h in range(n_heads):
        sl = slice(h * HEAD_DIM, (h + 1) * HEAD_DIM)
        qr = _rope128(q_ref[:, sl], cos2, sin2)
        kr = _rope128(k_ref[:, sl], cos2, sin2) * QK_SCALE
        vb = v_ref[:, sl].astype(BF16)
        a = lax.dot_general(qr.astype(BF16), kr.astype(BF16), NT_DIMS, preferred_element_type=F32) * dm_ref[h]
        o = jnp.dot(a.astype(BF16), vb, preferred_element_type=F32)
        qs = (qr * qd_ref[:, sl]).astype(BF16)
        ks = kr * kd_ref[:, sl]
        for i in range(n_sub):
            s_old = load_state(i, h)
            inter = jnp.dot(qs, s_old.astype(BF16), preferred_element_type=F32)
            if n_sub == 1:
                o = o + inter
                ks_i = ks
            else:
                sel = rowseq == float(i)
                o = o + jnp.where(sel, inter, 0.0)
                ks_i = jnp.where(sel, ks, 0.0)
            kv = lax.dot_general(ks_i.astype(BF16), vb, TN_DIMS, preferred_element_type=F32)
            store_state(i, h, s_old * cdec[h] + kv)
        yn = _head_ln(o, gn_ref[:, sl])
        y_ref[:, sl] = (jax.nn.silu(g_ref[:, sl]) * yn).astype(BF16)


def _ret_prompt_kernel(q_ref, k_ref, v_ref, g_ref, cos_ref, sin_ref, dm_ref, qd_ref, kd_ref, gn_ref,
                       y_ref, so_ref, s_ref, *, n_heads, cdec):
    c = pl.program_id(1)

    @pl.when(c == 0)
    def _():
        s_ref[...] = jnp.zeros_like(s_ref)

    def load_state(i, h):
        return s_ref[h]

    def store_state(i, h, val):
        s_ref[h] = val

    _ret_block(q_ref, k_ref, v_ref, g_ref, cos_ref, sin_ref, dm_ref, qd_ref, kd_ref, gn_ref, y_ref,
               load_state, store_state, None, n_heads, 1, cdec)

    @pl.when(c == pl.num_programs(1) - 1)
    def _():
        so_ref[...] = s_ref[...]


def _ret_sample_kernel(q_ref, k_ref, v_ref, g_ref, cos_ref, sin_ref, dm_ref, qd_ref, kd_ref, gn_ref,
                       rs_ref, s0_ref, y_ref, so_ref, *, n_heads, n_sub, cdec):
    def load_state(i, h):
        return s0_ref[i, h]

    def store_state(i, h, val):
        so_ref[i, h] = val

    _ret_block(q_ref, k_ref, v_ref, g_ref, cos_ref, sin_ref, dm_ref, qd_ref, kd_ref, gn_ref, y_ref,
               load_state, store_state, rs_ref[...], n_heads, n_sub, cdec)


def _ret_decay_tables(n_heads, chunk, n_sub):
    lg = np.log1p(-np.exp2(-5.0 - np.arange(n_heads, dtype=np.float64)))
    idx = np.arange(chunk, dtype=np.float64)
    diff = idx[:, None] - idx[None, :]
    dm1 = np.where(diff[None] >= 0, np.exp(np.maximum(diff, 0.0)[None] * lg[:, None, None]), 0.0)
    rows = n_sub * chunk
    dm = np.zeros((n_heads, rows, rows))
    for i in range(n_sub):
        dm[:, i * chunk:(i + 1) * chunk, i * chunk:(i + 1) * chunk] = dm1
    q_dec = np.exp((idx[:, None] + 1.0) * lg[None, :])
    k_dec = np.exp((chunk - 1.0 - idx)[:, None] * lg[None, :])
    qd = np.tile(np.repeat(q_dec, HEAD_DIM, axis=1), (n_sub, 1))
    kd = np.tile(np.repeat(k_dec, HEAD_DIM, axis=1), (n_sub, 1))
    cdec = tuple(float(np.float32(v)) for v in np.exp(chunk * lg))
    return jnp.asarray(dm, F32), jnp.asarray(qd, F32), jnp.asarray(kd, F32), cdec


def retention_prompt(proj, cos2, sin2, gn, B, L, n_heads):
    W = n_heads * HEAD_DIM
    C = min(CHUNK, L)
    nc = L // C
    dm, qd, kd, cdec = _ret_decay_tables(n_heads, C, 1)
    kern = functools.partial(_ret_prompt_kernel, n_heads=n_heads, cdec=cdec)
    col = lambda j: pl.BlockSpec((C, W), lambda b, c, j=j: (b * nc + c, j))
    const2 = lambda shape: pl.BlockSpec(shape, lambda b, c: (0,) * len(shape))
    return _Part(
        kern,
        args=(proj, proj, proj, proj, cos2, sin2, dm, qd, kd, gn),
        in_specs=[col(0), col(1), col(2), col(3),
                  pl.BlockSpec((C, HEAD_DIM), lambda b, c: (b * nc + c, 0)),
                  pl.BlockSpec((C, HEAD_DIM), lambda b, c: (b * nc + c, 0)),
                  const2(dm.shape), const2(qd.shape), const2(kd.shape), const2((1, W))],
        out_shape=[jax.ShapeDtypeStruct((B * L, W), BF16),
                   jax.ShapeDtypeStruct((B, n_heads, HEAD_DIM, HEAD_DIM), F32)],
        out_specs=[pl.BlockSpec((C, W), lambda b, c: (b * nc + c, 0)),
                   pl.BlockSpec((None, n_heads, HEAD_DIM, HEAD_DIM), lambda b, c: (b, 0, 0, 0))],
        scratch=[pltpu.VMEM((n_heads, HEAD_DIM, HEAD_DIM), F32)])


def retention_sample(proj, cos2, sin2, gn, state, layer, row0, DB, T, n_heads, G):
    W = n_heads * HEAD_DIM
    R = G * T
    rb0 = row0 // R
    dm, qd, kd, cdec = _ret_decay_tables(n_heads, T, G)
    rowseq = jnp.asarray(np.repeat(np.arange(G, dtype=np.float32), T)[:, None] * np.ones((1, HEAD_DIM), np.float32))
    kern = functools.partial(_ret_sample_kernel, n_heads=n_heads, n_sub=G, cdec=cdec)
    col = lambda j: pl.BlockSpec((R, W), lambda g, j=j: (rb0 + g, j))
    const1 = lambda shape: pl.BlockSpec(shape, lambda g: (0,) * len(shape))
    return _Part(
        kern,
        args=(proj, proj, proj, proj, cos2, sin2, dm, qd, kd, gn, rowseq, state),
        in_specs=[col(0), col(1), col(2), col(3),
                  pl.BlockSpec((R, HEAD_DIM), lambda g: (rb0 + g, 0)),
                  pl.BlockSpec((R, HEAD_DIM), lambda g: (rb0 + g, 0)),
                  const1(dm.shape), const1(qd.shape), const1(kd.shape), const1((1, W)),
                  const1(rowseq.shape),
                  pl.BlockSpec((None, G, n_heads, HEAD_DIM, HEAD_DIM), lambda g: (layer, g, 0, 0, 0))],
        out_shape=[jax.ShapeDtypeStruct((DB * T, W), BF16),
                   jax.ShapeDtypeStruct((DB, n_heads, HEAD_DIM, HEAD_DIM), F32)],
        out_specs=[pl.BlockSpec((R, W), lambda g: (g, 0)),
                   pl.BlockSpec((G, n_heads, HEAD_DIM, HEAD_DIM), lambda g: (g, 0, 0, 0))],
        scratch=[])


def _mlstm_block(u, v_ref, o_ref, gates, mprev_ref, mout_ref, nrows, tri, eye, lastm, gn_ref, y_ref,
                 load_c, store_c, store_n, rowseq, n_heads, n_sub, seq_len):
    W = n_heads * HEAD_DIM
    for h in range(n_heads):
        sl = slice(h * HEAD_DIM, (h + 1) * HEAD_DIM)
        q = u[:, h * HEAD_DIM:(h + 1) * HEAD_DIM]
        k = u[:, W + h * HEAD_DIM:W + (h + 1) * HEAD_DIM] * QK_SCALE
        qb, kb = q.astype(BF16), k.astype(BF16)
        vb = v_ref[:, sl].astype(BF16)
        ig_col = gates[:, h:h + 1]
        lf_col = gates[:, n_heads + h:n_heads + h + 1]
        m_col = mprev_ref[:, h:h + 1]
        f_row = jnp.sum(jnp.where(eye, lf_col, 0.0), axis=0, keepdims=True)
        bt_col = jnp.sum(jnp.where(tri, f_row, 0.0), axis=1, keepdims=True)
        bt_row = jnp.sum(jnp.where(eye, bt_col, 0.0), axis=0, keepdims=True)
        it_row = jnp.sum(jnp.where(eye, ig_col, 0.0), axis=0, keepdims=True)
        logw = jnp.where(tri, bt_col - bt_row + it_row, NEG_INF)
        log_inter = bt_col + m_col
        m_i = jnp.maximum(log_inter, jnp.max(logw, axis=1, keepdims=True))
        w = jnp.exp(logw - m_i)
        a = jnp.exp(log_inter - m_i)
        s = lax.dot_general(qb, kb, NT_DIMS, preferred_element_type=F32) * w
        num = jnp.dot(s.astype(BF16), vb, preferred_element_type=F32)
        den = jnp.sum(s, axis=1, keepdims=True) + a * jnp.sum(q * nrows(h), axis=1, keepdims=True)
        if n_sub == 1:
            rows = m_i.shape[0]
            m_last = jnp.broadcast_to(m_i[rows - 1:rows, :], m_i.shape)
            bt_last = jnp.broadcast_to(bt_col[rows - 1:rows, :], m_i.shape)
        else:
            m_row = jnp.sum(jnp.where(eye, m_i, 0.0), axis=0, keepdims=True)
            m_last = jnp.sum(jnp.where(lastm, m_row, 0.0), axis=1, keepdims=True)
            bt_last = jnp.sum(jnp.where(lastm, bt_row, 0.0), axis=1, keepdims=True)
        wl = jnp.exp(bt_last - bt_col + ig_col - m_last)
        al = jnp.exp(bt_last + m_col - m_last)
        kw = k * wl
        inter_sum = None
        for i in range(n_sub):
            c_old = load_c(i, h)
            inter = jnp.dot(qb, c_old.astype(BF16), preferred_element_type=F32)
            if n_sub == 1:
                kw_i = kw
            else:
                sel = rowseq == float(i)
                inter = jnp.where(sel, inter, 0.0)
                kw_i = jnp.where(sel, kw, 0.0)
            inter_sum = inter if inter_sum is None else inter_sum + inter
            al_i = al[i * seq_len:i * seq_len + 1, :]
            kv = lax.dot_general(kw_i.astype(BF16), vb, TN_DIMS, preferred_element_type=F32)
            store_c(i, h, al_i * c_old + kv)
            store_n(i, h, al_i, jnp.sum(kw_i, axis=0, keepdims=True))
        num = num + a * inter_sum
        hh = num / jnp.maximum(jnp.abs(den), jnp.exp(-m_i))
        yn = _head_ln(hh, gn_ref[:, sl])
        y_ref[:, sl] = (jax.nn.sigmoid(o_ref[:, sl]) * yn).astype(BF16)
        mout_ref[:, h:h + 1] = m_last


def _gate_values(g_ref, gb_ref, n_heads):
    pre = g_ref[...] + gb_ref[...]
    lane = lax.broadcasted_iota(jnp.int32, pre.shape, 1)
    lf = jnp.minimum(pre, 0.0) - jnp.log1p(jnp.exp(-jnp.abs(pre)))
    return jnp.where(lane < n_heads, pre, lf)


def _conv_silu(taps, cw_ref, cb_ref):
    acc = cb_ref[...]
    for j in range(CONV_W):
        acc = acc + taps[j] * cw_ref[j:j + 1, :]
    return jax.nn.silu(acc)


def _mlstm_prompt_kernel(x_ref, v_ref, o_ref, g_ref, cw_ref, cb_ref, gb_ref, gn_ref, tri_ref, eye_ref,
                         last_ref, y_ref, co_ref, no_ref, mo_ref,
                         xbuf_ref, c_ref, n_ref, m_ref, *, n_heads):
    c = pl.program_id(1)
    C = x_ref.shape[0]

    @pl.when(c == 0)
    def _():
        xbuf_ref[0:8, :] = jnp.zeros((8, xbuf_ref.shape[1]), F32)
        c_ref[...] = jnp.zeros_like(c_ref)
        n_ref[...] = jnp.zeros_like(n_ref)
        m_ref[...] = jnp.zeros_like(m_ref)

    xbuf_ref[8:8 + C, :] = x_ref[...]
    taps = [xbuf_ref[8 - (CONV_W - 1) + j:8 - (CONV_W - 1) + j + C, :] for j in range(CONV_W)]
    u = _conv_silu(taps, cw_ref, cb_ref)
    xbuf_ref[0:8, :] = x_ref[C - 8:C, :]
    gates = _gate_values(g_ref, gb_ref, n_heads)
    tri, eye, lastm = tri_ref[...] > 0.5, eye_ref[...] > 0.5, last_ref[...] > 0.5

    def load_c(i, h):
        return c_ref[h]

    def store_c(i, h, val):
        c_ref[h] = val

    def store_n(i, h, al_i, ksum):
        n_ref[h] = al_i * n_ref[h] + ksum

    def nrows(h):
        return n_ref[h]

    _mlstm_block(u, v_ref, o_ref, gates, m_ref, m_ref, nrows, tri, eye, lastm, gn_ref, y_ref,
                 load_c, store_c, store_n, None, n_heads, 1, C)

    @pl.when(c == pl.num_programs(1) - 1)
    def _():
        co_ref[...] = c_ref[...]
        no_ref[...] = n_ref[...]
        mo_ref[...] = m_ref[0:8, :]


def _mlstm_sample_kernel(x_ref, v_ref, o_ref, g_ref, cw_ref, cb_ref, gb_ref, gn_ref, tri_ref, eye_ref,
                         last_ref, rs_ref, rt_ref, st_ref, c0_ref, n0_ref, nr_ref, m0_ref,
                         y_ref, co_ref, no_ref, mo_ref, *, n_heads, n_sub, seq_len):
    R = x_ref.shape[0]
    x = x_ref[...]
    st = st_ref[...]
    tpos = rt_ref[:, 0:1]
    taps = []
    for j in range(CONV_W):
        s = CONV_W - 1 - j
        if s == 0:
            taps.append(x)
        else:
            cur = pltpu.roll(x, s, 0)
            old = pltpu.roll(st, (s - seq_len) % R, 0)
            taps.append(jnp.where(tpos >= float(s), cur, old))
    u = _conv_silu(taps, cw_ref, cb_ref)
    gates = _gate_values(g_ref, gb_ref, n_heads)
    tri, eye, lastm = tri_ref[...] > 0.5, eye_ref[...] > 0.5, last_ref[...] > 0.5
    mo_ref[...] = jnp.zeros_like(mo_ref)

    def load_c(i, h):
        return c0_ref[i, h]

    def store_c(i, h, val):
        co_ref[i, h] = val

    def store_n(i, h, al_i, ksum):
        sl = slice(h * HEAD_DIM, (h + 1) * HEAD_DIM)
        no_ref[i:i + 1, sl] = al_i * n0_ref[i:i + 1, sl] + ksum

    def nrows(h):
        return nr_ref[:, h * HEAD_DIM:(h + 1) * HEAD_DIM]

    _mlstm_block(u, v_ref, o_ref, gates, m0_ref, mo_ref, nrows, tri, eye, lastm, gn_ref, y_ref,
                 load_c, store_c, store_n, rs_ref[...], n_heads, n_sub, seq_len)


def _seq_masks(n_sub, seq_len):
    rows = n_sub * seq_len
    r = np.arange(rows)
    seq, t = r // seq_len, r % seq_len
    same = seq[:, None] == seq[None, :]
    tri = (same & (t[None, :] <= t[:, None])).astype(np.float32)
    eye = np.eye(rows, dtype=np.float32)
    last = (r[None, :] == (seq[:, None] * seq_len + seq_len - 1)).astype(np.float32)
    return jnp.asarray(tri), jnp.asarray(eye), jnp.asarray(last)


def mlstm_prompt(proj, conv_w, conv_b, gate_b, gn, B, L, n_heads, gate_blk):
    W = n_heads * HEAD_DIM
    C = min(CHUNK, L)
    nc = L // C
    tri, eye, last = _seq_masks(1, C)
    kern = functools.partial(_mlstm_prompt_kernel, n_heads=n_heads)
    const2 = lambda shape: pl.BlockSpec(shape, lambda b, c: (0,) * len(shape))
    return _Part(
        kern,
        args=(proj, proj, proj, proj, conv_w, conv_b, gate_b, gn, tri, eye, last),
        in_specs=[pl.BlockSpec((C, 2 * W), lambda b, c: (b * nc + c, 2)),
                  pl.BlockSpec((C, W), lambda b, c: (b * nc + c, 6)),
                  pl.BlockSpec((C, W), lambda b, c: (b * nc + c, 7)),
                  pl.BlockSpec((C, HEAD_DIM), lambda b, c: (b * nc + c, gate_blk)),
                  const2((CONV_W, 2 * W)), const2((1, 2 * W)), const2((1, HEAD_DIM)), const2((1, W)),
                  const2(tri.shape), const2(eye.shape), const2(last.shape)],
        out_shape=[jax.ShapeDtypeStruct((B * L, W), BF16),
                   jax.ShapeDtypeStruct((B, n_heads, HEAD_DIM, HEAD_DIM), F32),
                   jax.ShapeDtypeStruct((B, n_heads, 1, HEAD_DIM), F32),
                   jax.ShapeDtypeStruct((B, 8, HEAD_DIM), F32)],
        out_specs=[pl.BlockSpec((C, W), lambda b, c: (b * nc + c, 0)),
                   pl.BlockSpec((None, n_heads, HEAD_DIM, HEAD_DIM), lambda b, c: (b, 0, 0, 0)),
                   pl.BlockSpec((None, n_heads, 1, HEAD_DIM), lambda b, c: (b, 0, 0, 0)),
                   pl.BlockSpec((None, 8, HEAD_DIM), lambda b, c: (b, 0, 0))],
        scratch=[pltpu.VMEM((8 + C, 2 * W), F32),
                 pltpu.VMEM((n_heads, HEAD_DIM, HEAD_DIM), F32),
                 pltpu.VMEM((n_heads, 1, HEAD_DIM), F32),
                 pltpu.VMEM((C, HEAD_DIM), F32)])


def mlstm_sample(proj, conv_w, conv_b, gate_b, gn, conv_rows, c_state, n_state, n_rows, m_rows,
                 layer, row0, DB, T, n_heads, gate_blk, G):
    W = n_heads * HEAD_DIM
    R = G * T
    rb0 = row0 // R
    tri, eye, last = _seq_masks(G, T)
    ones = np.ones((1, HEAD_DIM), np.float32)
    rowseq = jnp.asarray(np.repeat(np.arange(G, dtype=np.float32), T)[:, None] * ones)
    rowt = jnp.asarray(np.tile(np.arange(T, dtype=np.float32), G)[:, None] * ones)
    kern = functools.partial(_mlstm_sample_kernel, n_heads=n_heads, n_sub=G, seq_len=T)
    const1 = lambda shape: pl.BlockSpec(shape, lambda g: (0,) * len(shape))
    return _Part(
        kern,
        args=(proj, proj, proj, proj, conv_w, conv_b, gate_b, gn, tri, eye, last, rowseq, rowt,
              conv_rows, c_state, n_state, n_rows, m_rows),
        out_shape=[jax.ShapeDtypeStruct((DB * T, W), BF16),
                   jax.ShapeDtypeStruct((DB, n_heads, HEAD_DIM, HEAD_DIM), F32),
                   jax.ShapeDtypeStruct((DB, W), F32),
                   jax.ShapeDtypeStruct((DB * T, HEAD_DIM), F32)],
        in_specs=[pl.BlockSpec((R, 2 * W), lambda g: (rb0 + g, 2)),
                  pl.BlockSpec((R, W), lambda g: (rb0 + g, 6)),
                  pl.BlockSpec((R, W), lambda g: (rb0 + g, 7)),
                  pl.BlockSpec((R, HEAD_DIM), lambda g: (rb0 + g, gate_blk)),
                  const1((CONV_W, 2 * W)), const1((1, 2 * W)), const1((1, HEAD_DIM)), const1((1, W)),
                  const1(tri.shape), const1(eye.shape), const1(last.shape),
                  const1(rowseq.shape), const1(rowt.shape),
                  pl.BlockSpec((None, R, 2 * W), lambda g: (layer, g, 0)),
                  pl.BlockSpec((None, G, n_heads, HEAD_DIM, HEAD_DIM), lambda g: (layer, g, 0, 0, 0)),
                  pl.BlockSpec((None, G, W), lambda g: (layer, g, 0)),
                  pl.BlockSpec((None, R, W), lambda g: (layer, g, 0)),
                  pl.BlockSpec((None, R, HEAD_DIM), lambda g: (layer, g, 0))],
        out_specs=[pl.BlockSpec((R, W), lambda g: (g, 0)),
                   pl.BlockSpec((G, n_heads, HEAD_DIM, HEAD_DIM), lambda g: (g, 0, 0, 0)),
                   pl.BlockSpec((G, W), lambda g: (g, 0)),
                   pl.BlockSpec((R, HEAD_DIM), lambda g: (g, 0))],
        scratch=[])


def _rope64x2(x, cos4, sin_lo, sin_hi):
    return x * cos4 + pltpu.roll(x, 96, 1) * sin_lo + pltpu.roll(x, 32, 1) * sin_hi


def _mla_prep_kernel(qn_ref, qp_ref, kv_ref, cos_ref, slo_ref, shi_ref, kvn_ref, wuk_ref,
                     kvr_ref, kb_ref, q_ref, *, n_heads, lat):
    cos4, slo, shi = cos_ref[...], slo_ref[...], shi_ref[...]
    c_new = _rms(kv_ref[:, 0:lat], kvn_ref[...])
    kr = _rope64x2(kv_ref[:, lat:lat + HEAD_DIM], cos4, slo, shi)
    kvr_ref[:, 0:lat] = c_new
    kvr_ref[:, lat:lat + HEAD_DIM] = kr
    kb_ref[:, 0:lat] = c_new.astype(BF16)
    kb_ref[:, lat:lat + HEAD_DIM] = (kr + pltpu.roll(kr, HEAD_DIM // 2, 1)).astype(BF16)
    qw = lat + HEAD_DIM
    lane = lax.broadcasted_iota(jnp.int32, (qp_ref.shape[0], HEAD_DIM), 1)
    for p in range(n_heads // 2):
        pe = _rope64x2(qp_ref[:, p * HEAD_DIM:(p + 1) * HEAD_DIM], cos4, slo, shi)
        for half in range(2):
            h = 2 * p + half
            q_lat = jnp.dot(qn_ref[:, h * HEAD_DIM:(h + 1) * HEAD_DIM].astype(BF16), wuk_ref[h],
                            preferred_element_type=F32)
            q_ref[:, h * qw:h * qw + lat] = q_lat.astype(BF16)
            keep = (lane < HEAD_DIM // 2) if half == 0 else (lane >= HEAD_DIM // 2)
            q_ref[:, h * qw + lat:(h + 1) * qw] = jnp.where(keep, pe, 0.0).astype(BF16)


def mla_prep(proj, cos4, sin_lo, sin_hi, kv_norm, w_uk_t, n_heads, lat):
    M = proj.shape[0]
    tm = _tile(M, 272, 16)
    qw = lat + HEAD_DIM
    kern = functools.partial(_mla_prep_kernel, n_heads=n_heads, lat=lat)
    row = lambda w, j: pl.BlockSpec((tm, w), lambda i, j=j: (i, j))
    const1 = lambda shape: pl.BlockSpec(shape, lambda i: (0,) * len(shape))
    nope_w, pe_w = n_heads * HEAD_DIM, n_heads * HEAD_DIM // 2
    return pl.pallas_call(
        kern,
        out_shape=(jax.ShapeDtypeStruct((M, qw), F32),
                   jax.ShapeDtypeStruct((M, qw), BF16),
                   jax.ShapeDtypeStruct((M, n_heads * qw), BF16)),
        grid=(M // tm,),
        in_specs=[row(nope_w, 4), row(pe_w, 10), row(qw, 15),
                  row(HEAD_DIM, 0), row(HEAD_DIM, 0), row(HEAD_DIM, 0),
                  const1((1, lat)), const1(w_uk_t.shape)],
        out_specs=(row(qw, 0), row(qw, 0), row(n_heads * qw, 0)),
        compiler_params=_params(("parallel",), 32),
        name="mla_prep",
    )(proj, proj, proj, cos4, sin_lo, sin_hi, kv_norm, w_uk_t)


def _mla_prompt_kernel(q_ref, kb_ref, o_ref, qt_ref, m_ref, l_ref, acc_ref, *, n_heads, lat, tq, scale):
    i = pl.program_id(1)
    qw = lat + HEAD_DIM
    cols = n_heads * tq
    for h in range(n_heads):
        qt_ref[:, h * tq:(h + 1) * tq] = q_ref[:, h * qw:(h + 1) * qw].astype(F32).T.astype(BF16)
    m_ref[...] = jnp.full(m_ref.shape, NEG_INF, F32)
    l_ref[...] = jnp.zeros_like(l_ref)
    acc_ref[...] = jnp.zeros_like(acc_ref)

    def chunk(j, masked):
        kblk = kb_ref[pl.ds(pl.multiple_of(j * tq, tq), tq), :]
        s = jnp.dot(kblk, qt_ref[...], preferred_element_type=F32) * scale
        if masked:
            kloc = lax.broadcasted_iota(jnp.int32, (tq, cols), 0)
            qloc = lax.broadcasted_iota(jnp.int32, (tq, cols), 1) & (tq - 1)
            s = jnp.where(kloc <= qloc, s, NEG_INF)
        m_old = m_ref[...]
        m_new = jnp.maximum(m_old, jnp.max(s, axis=0, keepdims=True))
        alpha = jnp.exp(m_old - m_new)
        p = jnp.exp(s - m_new)
        l_ref[...] = alpha * l_ref[...] + jnp.sum(p, axis=0, keepdims=True)
        m_ref[...] = m_new
        vt = kblk[:, 0:lat].astype(F32).T.astype(BF16)
        acc_ref[...] = alpha * acc_ref[...] + jnp.dot(vt, p.astype(BF16), preferred_element_type=F32)

    def body(j, carry):
        chunk(j, False)
        return carry

    lax.fori_loop(0, i, body, 0)
    chunk(i, True)
    ot = acc_ref[...] / l_ref[...]
    for h in range(n_heads):
        o_ref[:, h * lat:(h + 1) * lat] = ot[:, h * tq:(h + 1) * tq].T.astype(BF16)


def mla_prompt(q, kb, B, L, n_heads, lat, scale):
    qw = lat + HEAD_DIM
    tq = _tile(L, 256, 128)
    assert tq & (tq - 1) == 0
    nq = L // tq
    cols = n_heads * tq
    kern = functools.partial(_mla_prompt_kernel, n_heads=n_heads, lat=lat, tq=tq, scale=scale)
    return pl.pallas_call(
        kern,
        out_shape=jax.ShapeDtypeStruct((B * L, n_heads * lat), BF16),
        grid=(B, nq),
        in_specs=[pl.BlockSpec((tq, n_heads * qw), lambda b, i: (b * nq + i, 0)),
                  pl.BlockSpec((L, qw), lambda b, i: (b, 0))],
        out_specs=pl.BlockSpec((tq, n_heads * lat), lambda b, i: (b * nq + i, 0)),
        scratch_shapes=[pltpu.VMEM((qw, cols), BF16),
                        pltpu.VMEM((1, cols), F32),
                        pltpu.VMEM((1, cols), F32),
                        pltpu.VMEM((lat, cols), F32)],
        compiler_params=_params(("parallel", "arbitrary"), 48),
        name="mla_prompt",
    )(q, kb)


def _mla_sample_kernel(pt_ref, q_ref, kn_ref, kv_hbm, krt_hbm, o_ref, kvbuf, krbuf, cbuf_ref, rbuf_ref, sem,
                       *, layer, n_pages, lat, n_heads, scale):
    b = pl.program_id(0)
    nb = pl.num_programs(0)
    slot = b & 1
    page = kvbuf.shape[2]
    rope_w = krbuf.shape[2]

    def page_copies(seq, dst_slot):
        copies = []
        for p in range(n_pages):
            pg = pt_ref[seq * n_pages + p]
            copies.append(pltpu.make_async_copy(kv_hbm.at[layer, pg], kvbuf.at[dst_slot, p], sem.at[0, dst_slot]))
            copies.append(pltpu.make_async_copy(krt_hbm.at[layer, pg], krbuf.at[dst_slot, p], sem.at[1, dst_slot]))
        return copies

    @pl.when(b == 0)
    def _():
        for cp in page_copies(0, 0):
            cp.start()

    for cp in page_copies(b, slot):
        cp.wait()
    nxt = jnp.minimum(b + 1, nb - 1)
    for cp in page_copies(nxt, 1 - slot):
        cp.start()

    for p in range(n_pages):
        cbuf_ref[p * page:(p + 1) * page, :] = kvbuf[slot, p].astype(BF16)
        rbuf_ref[:, p * page:(p + 1) * page] = krbuf[slot, p].astype(BF16)
    q = q_ref[...]
    cb = cbuf_ref[...]
    q_pe = q[:, lat:lat + rope_w] + q[:, lat + rope_w:]
    s = (lax.dot_general(q[:, 0:lat], cb, NT_DIMS, preferred_element_type=F32)
         + jnp.dot(q_pe, rbuf_ref[...], preferred_element_type=F32)) * scale
    kn = kn_ref[...]
    sn = lax.dot_general(q, kn, NT_DIMS, preferred_element_type=F32) * scale
    t_q = jnp.right_shift(lax.broadcasted_iota(jnp.int32, sn.shape, 0), n_heads.bit_length() - 1)
    t_k = lax.broadcasted_iota(jnp.int32, sn.shape, 1)
    sn = jnp.where(t_k <= t_q, sn, NEG_INF)
    m = jnp.maximum(jnp.max(s, axis=1, keepdims=True), jnp.max(sn, axis=1, keepdims=True))
    pc = jnp.exp(s - m)
    pn = jnp.exp(sn - m)
    denom = jnp.sum(pc, axis=1, keepdims=True) + jnp.sum(pn, axis=1, keepdims=True)
    acc = (jnp.dot(pc.astype(BF16), cb, preferred_element_type=F32)
           + jnp.dot(pn.astype(BF16), kn[:, 0:lat], preferred_element_type=F32))
    o_ref[...] = (acc / denom).astype(BF16)

    @pl.when(b == nb - 1)
    def _():
        for cp in page_copies(nxt, 1 - slot):
            cp.wait()


def mla_sample(q_s, kb_new, cache_kv, cache_krt, page_table, layer, n_heads, lat, scale):
    DB, rows, qw = q_s.shape
    n_pages = page_table.shape[1]
    page = cache_kv.shape[2]
    rope_w = cache_krt.shape[2]
    seq_len = rows // n_heads
    assert seq_len <= page and n_heads & (n_heads - 1) == 0 and 2 * rope_w == HEAD_DIM
    assert n_pages <= MAX_PAGES_IN_VMEM, "a sequence's cache pages are held in VMEM twice (two slots)"
    kern = functools.partial(_mla_sample_kernel, layer=layer, n_pages=n_pages, lat=lat, n_heads=n_heads,
                             scale=scale)
    grid_spec = pltpu.PrefetchScalarGridSpec(
        num_scalar_prefetch=1,
        grid=(DB,),
        in_specs=[pl.BlockSpec((None, rows, qw), lambda b, pt: (b, 0, 0)),
                  pl.BlockSpec((None, page, qw), lambda b, pt: (b, 0, 0)),
                  pl.BlockSpec(memory_space=pl.ANY),
                  pl.BlockSpec(memory_space=pl.ANY)],
        out_specs=pl.BlockSpec((None, rows, lat), lambda b, pt: (b, 0, 0)),
        scratch_shapes=[pltpu.VMEM((2, n_pages, page, lat), F32),
                        pltpu.VMEM((2, n_pages, rope_w, page), F32),
                        pltpu.VMEM((n_pages * page, lat), BF16),
                        pltpu.VMEM((rope_w, n_pages * page), BF16),
                        pltpu.SemaphoreType.DMA((2, 2))])
    return pl.pallas_call(
        kern,
        out_shape=jax.ShapeDtypeStruct((DB, rows, lat), BF16),
        grid_spec=grid_spec,
        compiler_params=_params(("arbitrary",), 56),
        name="mla_sample",
    )(page_table.reshape(-1), q_s, kb_new, cache_kv, cache_krt)


def _rope_tables(pos, dim):
    inv = ROPE_BASE ** (-jnp.arange(0, dim, 2, dtype=F32) / dim)
    ang = pos.astype(F32)[:, None] * inv[None, :]
    return jnp.cos(ang), jnp.sin(ang)


def _relayout_w_in(w_in, n_mla, nope, rope_w, lat):
    depth, D, _ = w_in.shape
    main = 4 * (D // 4) + D
    n_gate = 2 * (D // (4 * HEAD_DIM))
    w = w_in.astype(BF16)
    gates = w[..., main:main + n_gate]
    q0 = main + n_gate
    mlq = w[..., q0:q0 + n_mla * (nope + rope_w)].reshape(depth, D, n_mla, nope + rope_w)
    q_nope = mlq[..., :nope].reshape(depth, D, n_mla * nope)
    q_pe = mlq[..., nope:].reshape(depth, D, n_mla * rope_w)
    k0 = q0 + n_mla * (nope + rope_w)
    mkv = w[..., k0:k0 + lat + rope_w]
    zeros = lambda n: jnp.zeros((depth, D, n), BF16)
    return jnp.concatenate([w[..., :main], q_nope, q_pe, gates, zeros(HEAD_DIM - n_gate),
                            mkv, zeros(HEAD_DIM - rope_w)], axis=-1)


def kernel(x_prompt, x_sample, cache_mla_kv, cache_mla_kr, page_table, state_ret, state_mlstm_C, state_mlstm_n, state_mlstm_m, state_mlstm_conv, norm_attn, norm_mlp, norm_final, w_in, ret_gn, mlstm_conv_w, mlstm_conv_b, mlstm_gate_b, mlstm_gn, mla_kv_norm, w_uk, w_uv, w_out, w_up, w_down):
    B, L, D = x_prompt.shape
    DB, T, _ = x_sample.shape
    depth = w_in.shape[0]
    n_heads = D // (4 * HEAD_DIM)
    W = n_heads * HEAD_DIM
    n_mla = D // (2 * HEAD_DIM)
    lat = mla_kv_norm.shape[1]
    rope_w = cache_mla_kr.shape[3]
    page = cache_mla_kv.shape[2]
    past_len = page_table.shape[1] * page
    assert D == 2048 and lat == 2 * HEAD_DIM and rope_w == HEAD_DIM // 2 and T >= CONV_W - 1 and L % min(CHUNK, L) == 0
    mla_scale = (HEAD_DIM + rope_w) ** -0.5
    G = 8
    n_p, n_s = B * L, DB * T
    gate_blk = (4 * W + 4 * W + n_mla * HEAD_DIM + n_mla * rope_w) // HEAD_DIM
    assert n_p % (G * T) == 0 and DB % G == 0

    pos = jnp.concatenate([jnp.tile(jnp.arange(L, dtype=jnp.int32), B),
                           jnp.tile(past_len + jnp.arange(T, dtype=jnp.int32), DB)])
    cos_h, sin_h = _rope_tables(pos, HEAD_DIM)
    cos2 = jnp.concatenate([cos_h, cos_h], axis=1)
    sin2 = jnp.concatenate([-sin_h, sin_h], axis=1)
    cos_r, sin_r = _rope_tables(pos, rope_w)
    zero_r = jnp.zeros_like(sin_r)
    cos4 = jnp.concatenate([cos_r] * 4, axis=1)
    sin_lo = jnp.concatenate([-sin_r, zero_r, -sin_r, zero_r], axis=1)
    sin_hi = jnp.concatenate([zero_r, sin_r, zero_r, sin_r], axis=1)

    w_in_b = _relayout_w_in(w_in, n_mla, HEAD_DIM, rope_w, lat)
    w_out_b, w_up_b, w_down_b = w_out.astype(BF16), w_up.astype(BF16), w_down.astype(BF16)
    w_uk_t = jnp.transpose(w_uk, (0, 2, 3, 1)).astype(BF16)
    w_uv_t = jnp.transpose(w_uv, (0, 2, 1, 3)).astype(BF16)
    gate_b = jnp.pad(mlstm_gate_b, ((0, 0), (0, HEAD_DIM - mlstm_gate_b.shape[1])))

    conv_rows = jnp.pad(state_mlstm_conv, ((0, 0), (0, 0), (T - (CONV_W - 1), 0), (0, 0))).reshape(depth, n_s, 2 * W)
    n_state = state_mlstm_n.reshape(depth, DB, W)
    n_rows = jnp.repeat(n_state, T, axis=1)
    m_rows = jnp.repeat(jnp.pad(state_mlstm_m, ((0, 0), (0, 0), (0, HEAD_DIM - n_heads))), T, axis=1)

    cache_krt = jnp.swapaxes(cache_mla_kr, 2, 3)

    h = jnp.concatenate([x_prompt.reshape(n_p, D), x_sample.reshape(n_s, D)], axis=0)
    outs = [[] for _ in range(14)]
    for l in range(depth):
        proj = norm_matmul(h, norm_attn[l][None], w_in_b, l)

        gn_r = ret_gn[l][None]
        cw, cb, gb, gn_m = mlstm_conv_w[l], mlstm_conv_b[l][None], gate_b[l][None], mlstm_gn[l][None]
        (yr_p, ret_p), (ym_p, c_p, nn_p, mm_p) = _run_parts(
            [retention_prompt(proj, cos2, sin2, gn_r, B, L, n_heads),
             mlstm_prompt(proj, cw, cb, gb, gn_m, B, L, n_heads, gate_blk)],
            (B, L // min(CHUNK, L)), ("parallel", "arbitrary"), 40, "recurrent_prompt")
        (yr_s, ret_s), (ym_s, c_s, nn_s, mm_s) = _run_parts(
            [retention_sample(proj, cos2, sin2, gn_r, state_ret, l, n_p, DB, T, n_heads, G),
             mlstm_sample(proj, cw, cb, gb, gn_m, conv_rows, state_mlstm_C, n_state,
                          n_rows, m_rows, l, n_p, DB, T, n_heads, gate_blk, G)],
            (DB // G,), ("parallel",), 40, "recurrent_sample")

        kvr, kb, q = mla_prep(proj, cos4, sin_lo, sin_hi, mla_kv_norm[l][None], w_uk_t[l], n_mla, lat)
        o_p = mla_prompt(q, kb, B, L, n_mla, lat, mla_scale)
        q_s = q[n_p:].reshape(DB, T * n_mla, lat + HEAD_DIM)
        kb_new = jnp.pad(kb[n_p:].reshape(DB, T, lat + HEAD_DIM), ((0, 0), (0, page - T), (0, 0)))
        o_s = mla_sample(q_s, kb_new, cache_mla_kv, cache_krt, page_table, l, n_mla, lat, mla_scale)

        h = outproj_residual(h, (yr_p, yr_s), (ym_p, ym_s), (o_p, o_s.reshape(n_s, n_mla * lat)),
                             w_uv_t, w_out_b, l)
        h = mlp_residual(h, norm_mlp[l][None], w_up_b, w_down_b, l)

        conv_p = jnp.stack([proj[b * L + L - (CONV_W - 1):(b + 1) * L, 4 * W:6 * W] for b in range(B)])
        conv_s = proj[n_p:, 4 * W:6 * W].reshape(DB, T, 2 * W)[:, T - (CONV_W - 1):]
        per_layer = (
            kvr[:n_p, :lat].reshape(B, L, lat), kvr[:n_p, lat:lat + rope_w].reshape(B, L, rope_w),
            kvr[n_p:, :lat].reshape(DB, T, lat), kvr[n_p:, lat:lat + rope_w].reshape(DB, T, rope_w),
            ret_p, ret_s, c_p, c_s,
            nn_p.reshape(B, n_heads, HEAD_DIM), nn_s.reshape(DB, n_heads, HEAD_DIM),
            mm_p[:, 0, :n_heads], mm_s.reshape(DB, T, HEAD_DIM)[:, T - 1, :n_heads],
            conv_p, conv_s,
        )
        for acc, val in zip(outs, per_layer):
            acc.append(val)

    y_p, y_s = final_norm(h, norm_final[None], n_p)
    return (y_p.reshape(B, L, D), y_s.reshape(DB, T, D)) + tuple(jnp.stack(o) for o in outs)
```
